```python
import math
import jax, jax.numpy as jnp
from jax import lax
import numpy as np

D_MODEL = 1024
BATCH = 32
SEQ = 2048
DEPTH = 1

N_MEM = 256
MIX_WIDTH = D_MODEL
ATTN_WIDTH = MIX_WIDTH // 2
HEAD_DIM_ATTN = 64
N_ATTN_HEADS = ATTN_WIDTH // HEAD_DIM_ATTN
DILATED_BRANCHES = ((128, 1), (512, 4), (2048, 16))
BAND_BLOCK = 128
MLSTM_WIDTH = MIX_WIDTH - ATTN_WIDTH
N_MLSTM_HEADS = 4
HEAD_DIM_MLSTM = MLSTM_WIDTH // N_MLSTM_HEADS
MLSTM_CHUNK = 64
MLSTM_CONV = 4
IN_COLS = 3 * ATTN_WIDTH + 4 * MLSTM_WIDTH + 2 * N_MLSTM_HEADS
N_XATTN_HEADS = 4
HEAD_DIM_XATTN = D_MODEL // N_XATTN_HEADS
D_FF = ((8 * D_MODEL // 3 + 127) // 128) * 128
FFN_CONV = 3
RMS_EPS = 1e-6
NEG_INF = -1e30

kernel_name = 'hybrid_dilated_attn_mlstm_parallel_block'


def _rmsnorm(x, g):
    xf = x.astype(jnp.float32)
    y = xf * lax.rsqrt(jnp.mean(xf * xf, axis=-1, keepdims=True) + RMS_EPS)
    return (y * g.astype(jnp.float32)).astype(x.dtype)


def _causal_dwconv(x, w, b):
    K = w.shape[0]
    S = x.shape[1]
    xp = jnp.pad(x, ((0, 0), (K - 1, 0), (0, 0)))
    out = b
    for j in range(K):
        out = out + w[j] * xp[:, K - 1 - j:K - 1 - j + S]
    return out


def _alibi_slopes(n_heads):
    h = jnp.arange(1, n_heads + 1, dtype=jnp.float32)
    return jnp.exp2(-8.0 * h / n_heads)


def _dilated_branch(q, k, v, window, dilation, slopes):
    B, S, H, dh = q.shape
    span = window // dilation
    U = S // dilation
    nb = -(-U // BAND_BLOCK)
    Up = nb * BAND_BLOCK

    def to_sub(t):
        t = t.reshape(B, U, dilation, H, dh).transpose(0, 2, 3, 1, 4)
        return jnp.pad(t, ((0, 0), (0, 0), (0, 0), (0, Up - U), (0, 0)))

    def band(t):
        tp = jnp.pad(t, ((0, 0), (0, 0), (0, 0), (BAND_BLOCK, 0), (0, 0)))
        tp = tp.reshape(B, dilation, H, nb + 1, BAND_BLOCK, dh)
        return jnp.concatenate([tp[:, :, :, :-1], tp[:, :, :, 1:]], axis=4)

    qb = to_sub(q).reshape(B, dilation, H, nb, BAND_BLOCK, dh)
    kb = band(to_sub(k))
    vb = band(to_sub(v))
    s = jnp.einsum('brhnqd,brhnkd->brhnqk', qb, kb, preferred_element_type=jnp.float32)

    qi = jnp.arange(BAND_BLOCK)[:, None]
    kj = jnp.arange(2 * BAND_BLOCK)[None, :]
    rel = BAND_BLOCK + qi - kj
    in_band = (rel >= 0) & (rel <= span)
    blk = jnp.arange(nb)[:, None, None]
    valid = in_band[None] & ((blk > 0) | (kj[None] >= BAND_BLOCK))
    dist = (rel * dilation).astype(jnp.float32)
    bias = -slopes[:, None, None, None] * dist[None, None]
    s = jnp.where(valid, s + bias, NEG_INF)

    m = jnp.max(s, axis=-1, keepdims=True)
    p = jnp.exp(s - m)
    den = jnp.sum(p, axis=-1)
    o = jnp.einsum('brhnqk,brhnkd->brhnqd', p, vb.astype(jnp.float32)) / den[..., None]
    lse = m[..., 0] + jnp.log(den)

    o = o.reshape(B, dilation, H, Up, dh)[:, :, :, :U].transpose(0, 3, 1, 2, 4).reshape(B, S, H, dh)
    lse = lse.reshape(B, dilation, H, Up)[:, :, :, :U].transpose(0, 3, 1, 2).reshape(B, S, H)
    return o, lse


def _mlstm_chunkwise(q, k, v, log_i, log_f):
    B, S, H, dh = q.shape
    L = MLSTM_CHUNK
    nc = S // L

    def chunks(t):
        return t.reshape(B, nc, L, H, dh).transpose(0, 3, 1, 2, 4)

    def gchunks(t):
        return t.reshape(B, nc, L, H).transpose(0, 3, 1, 2)

    qc = chunks(q)
    kc = chunks(k) * (dh ** -0.5)
    vc = chunks(v)
    li = gchunks(log_i)
    b = jnp.cumsum(gchunks(log_f), axis=-1)
    g = b[..., -1]

    a = g[..., None] - b + li
    m_loc = jnp.max(a, axis=-1)
    wa = jnp.exp(a - m_loc[..., None])
    kv = jnp.einsum('bhcl,bhcld,bhcle->bhcde', wa, kc, vc)
    ksum = jnp.einsum('bhcl,bhcld->bhcd', wa, kc)

    def step(carry, inp):
        C, n, m = carry
        g_c, mloc_c, kv_c, ks_c = inp
        m_new = jnp.maximum(g_c + m, mloc_c)
        dec = jnp.exp(g_c + m - m_new)
        inj = jnp.exp(mloc_c - m_new)
        C_new = dec[..., None, None] * C + inj[..., None, None] * kv_c
        n_new = dec[..., None] * n + inj[..., None] * ks_c
        return (C_new, n_new, m_new), (C, n, m)

    init = (jnp.zeros((B, H, dh, dh), jnp.float32),
            jnp.zeros((B, H, dh), jnp.float32),
            jnp.zeros((B, H), jnp.float32))
    _, (C_prev, n_prev, m_prev) = lax.scan(
        step, init,
        (g.transpose(2, 0, 1), m_loc.transpose(2, 0, 1),
         kv.transpose(2, 0, 1, 3, 4), ksum.transpose(2, 0, 1, 3)))
    C_prev = C_prev.transpose(1, 2, 0, 3, 4)
    n_prev = n_prev.transpose(1, 2, 0, 3)
    m_prev = m_prev.transpose(1, 2, 0)

    causal = jnp.tril(jnp.ones((L, L), dtype=bool))
    D = jnp.where(causal, b[..., :, None] - b[..., None, :] + li[..., None, :], NEG_INF)
    inter_log = b + m_prev[..., None]
    m_t = jnp.maximum(jnp.max(D, axis=-1), inter_log)
    Sm = jnp.exp(D - m_t[..., None]) * jnp.einsum('bhctd,bhcsd->bhcts', qc, kc)
    inter_w = jnp.exp(inter_log - m_t)
    num = (inter_w[..., None] * jnp.einsum('bhctd,bhcde->bhcte', qc, C_prev)
           + jnp.einsum('bhcts,bhcse->bhcte', Sm, vc))
    den = inter_w * jnp.einsum('bhctd,bhcd->bhct', qc, n_prev) + jnp.sum(Sm, axis=-1)
    h = num / jnp.maximum(jnp.abs(den), jnp.exp(-m_t))[..., None]
    return h.transpose(0, 2, 3, 1, 4).reshape(B, S, H, dh)


def _parallel_mixer(xn, w_in, conv_w, conv_b, gate_b, head_g, w_out):
    B, S, _ = xn.shape
    A, M, Hm = ATTN_WIDTH, MLSTM_WIDTH, N_MLSTM_HEADS
    proj = xn @ w_in
    aq, ak, av, mqk, mv, mo, mgate = jnp.split(
        proj, [A, 2 * A, 3 * A, 3 * A + 2 * M, 3 * A + 3 * M, 3 * A + 4 * M], axis=-1)

    ashp = (B, S, N_ATTN_HEADS, HEAD_DIM_ATTN)
    aq = aq.reshape(ashp) * (HEAD_DIM_ATTN ** -0.5)
    ak = ak.reshape(ashp)
    av = av.reshape(ashp)
    slopes = _alibi_slopes(N_ATTN_HEADS)
    outs, lses = [], []
    for window, dil in DILATED_BRANCHES:
        o, lse = _dilated_branch(aq, ak, av, window, dil, slopes)
        outs.append(o)
        lses.append(lse)
    wts = jax.nn.softmax(jnp.stack(lses, axis=0), axis=0)
    attn = jnp.einsum('ibsh,ibshd->bshd', wts, jnp.stack(outs, axis=0))
    attn = attn.reshape(B, S, A).astype(xn.dtype)

    mqk = jax.nn.silu(_causal_dwconv(mqk, conv_w, conv_b))
    mq, mk = jnp.split(mqk, 2, axis=-1)
    mshp = (B, S, Hm, HEAD_DIM_MLSTM)
    pre = mgate.astype(jnp.float32) + gate_b.astype(jnp.float32)
    log_i = pre[..., :Hm]
    log_f = jax.nn.log_sigmoid(pre[..., Hm:])
    hm = _mlstm_chunkwise(mq.reshape(mshp).astype(jnp.float32),
                          mk.reshape(mshp).astype(jnp.float32),
                          mv.reshape(mshp).astype(jnp.float32), log_i, log_f)
    hm = hm * lax.rsqrt(jnp.mean(hm * hm, axis=-1, keepdims=True) + RMS_EPS)
    hm = hm * head_g.astype(jnp.float32).reshape(Hm, HEAD_DIM_MLSTM)
    hm = (hm.reshape(B, S, M) * jax.nn.sigmoid(mo.astype(jnp.float32))).astype(xn.dtype)

    return jnp.concatenate([attn, hm], axis=-1) @ w_out


def _memory_xattn(hn, memn, w_q, w_kv, w_o):
    B, S, _ = hn.shape
    q = (hn @ w_q).reshape(B, S, N_XATTN_HEADS, HEAD_DIM_XATTN) * (HEAD_DIM_XATTN ** -0.5)
    kv = (memn @ w_kv).reshape(B, memn.shape[1], 2, N_XATTN_HEADS, HEAD_DIM_XATTN)
    s = jnp.einsum('bshd,bmhd->bhsm', q, kv[:, :, 0], preferred_element_type=jnp.float32)
    p = jax.nn.softmax(s, axis=-1)
    o = jnp.einsum('bhsm,bmhd->bshd', p, kv[:, :, 1].astype(jnp.float32))
    return o.reshape(B, S, D_MODEL).astype(hn.dtype) @ w_o


def _conv_ffn(hn, w_up, conv_w, conv_b, w_down):
    gate, up = jnp.split(hn @ w_up, 2, axis=-1)
    gate = _causal_dwconv(gate, conv_w, conv_b)
    return (jax.nn.silu(gate) * up) @ w_down


def setup_inputs(seed: int = 0) -> dict:
    key = jax.random.key(seed)
    ks = jax.random.split(key, 24)
    f32 = jnp.float32
    L = DEPTH

    def nrm(k, shape, fan_in):
        return jax.random.normal(k, shape, f32) * (fan_in ** -0.5)

    def gain(k, shape):
        return 1.0 + 0.02 * jax.random.normal(k, shape, f32)

    def small(k, shape):
        return 0.02 * jax.random.normal(k, shape, f32)

    i_bias = 0.1 * jax.random.normal(ks[5], (L, N_MLSTM_HEADS), f32)
    f_bias = (jnp.linspace(3.0, 6.0, N_MLSTM_HEADS, dtype=f32)[None]
              + 0.1 * jax.random.normal(ks[6], (L, N_MLSTM_HEADS), f32))
    return {
        'x': jax.random.normal(ks[0], (BATCH, SEQ, D_MODEL), f32),
        'mem': jax.random.normal(ks[1], (BATCH, N_MEM, D_MODEL), f32),
        'norm_mix_g': gain(ks[2], (L, D_MODEL)),
        'w_in': nrm(ks[3], (L, D_MODEL, IN_COLS), D_MODEL),
        'mlstm_conv_w': nrm(ks[4], (L, MLSTM_CONV, 2 * MLSTM_WIDTH), MLSTM_CONV),
        'mlstm_conv_b': small(ks[7], (L, 2 * MLSTM_WIDTH)),
        'mlstm_gate_b': jnp.concatenate([i_bias, f_bias], axis=-1),
        'mlstm_head_g': gain(ks[8], (L, MLSTM_WIDTH)),
        'w_mix_out': nrm(ks[9], (L, MIX_WIDTH, D_MODEL), MIX_WIDTH),
        'norm_xattn_g': gain(ks[10], (L, D_MODEL)),
        'norm_mem_g': gain(ks[11], (L, D_MODEL)),
        'w_xq': nrm(ks[12], (L, D_MODEL, D_MODEL), D_MODEL),
        'w_xkv': nrm(ks[13], (L, D_MODEL, 2 * D_MODEL), D_MODEL),
        'w_xo': nrm(ks[14], (L, D_MODEL, D_MODEL), D_MODEL),
        'norm_ffn_g': gain(ks[15], (L, D_MODEL)),
        'w_ffn_up': nrm(ks[16], (L, D_MODEL, 2 * D_FF), D_MODEL),
        'ffn_conv_w': nrm(ks[17], (L, FFN_CONV, D_FF), FFN_CONV),
        'ffn_conv_b': small(ks[18], (L, D_FF)),
        'w_ffn_down': nrm(ks[19], (L, D_FF, D_MODEL), D_FF),
        'norm_final_g': gain(ks[20], (D_MODEL,)),
    }


def reference(x, mem, norm_mix_g, w_in, mlstm_conv_w, mlstm_conv_b, mlstm_gate_b,
              mlstm_head_g, w_mix_out, norm_xattn_g, norm_mem_g, w_xq, w_xkv, w_xo,
              norm_ffn_g, w_ffn_up, ffn_conv_w, ffn_conv_b, w_ffn_down, norm_final_g):
    h = x
    for l in range(DEPTH):
        h = h + _parallel_mixer(_rmsnorm(h, norm_mix_g[l]), w_in[l], mlstm_conv_w[l],
                                mlstm_conv_b[l], mlstm_gate_b[l], mlstm_head_g[l],
                                w_mix_out[l])
        h = h + _memory_xattn(_rmsnorm(h, norm_xattn_g[l]), _rmsnorm(mem, norm_mem_g[l]),
                              w_xq[l], w_xkv[l], w_xo[l])
        h = h + _conv_ffn(_rmsnorm(h, norm_ffn_g[l]), w_ffn_up[l], ffn_conv_w[l],
                          ffn_conv_b[l], w_ffn_down[l])
    return _rmsnorm(h, norm_final_g)
```

```python
import functools

import jax
import jax.numpy as jnp
from jax import lax
from jax.experimental import pallas as pl
from jax.experimental.pallas import tpu as pltpu

F32 = jnp.float32
BF16 = jnp.bfloat16

D_MODEL = 1024
ATTN_WIDTH = 512
HEAD_DIM_ATTN = 64
N_ATTN_HEADS = 8
DILATED_BRANCHES = ((128, 1), (512, 4), (2048, 16))
BAND_BLOCK = 128
MLSTM_WIDTH = 512
N_MLSTM_HEADS = 4
HEAD_DIM_MLSTM = 128
MLSTM_CONV = 4
N_XATTN_HEADS = 4
HEAD_DIM_XATTN = 256
D_FF = 2816
FFN_CONV = 3
RMS_EPS = 1e-6
NEG_INF = -1e30

GATE_PAD = 128
MLSTM_CHUNK = 256
ROW_TILE = 512
FFN_CHUNK = 256
FFN_ROWS = 256
MIB = 1024 * 1024


def _cparams(semantics, vmem_mib):
    return pltpu.CompilerParams(dimension_semantics=semantics, vmem_limit_bytes=vmem_mib * MIB)


def _rms(x, g):
    return x * lax.rsqrt(jnp.mean(x * x, axis=-1, keepdims=True) + RMS_EPS) * g


def _dot(a, b):
    return jnp.dot(a, b, preferred_element_type=F32)


def _dot_nt(a, b):
    return lax.dot_general(a, b, (((1,), (1,)), ((), ())), preferred_element_type=F32)


def _const_spec(shape):
    return pl.BlockSpec(shape, lambda *_: (0,) * len(shape))


def _inproj_kernel(x_ref, g_ref, w_ref, aqkv_ref, mqk_ref, mvo_ref, gates_ref):
    xn = _rms(x_ref[...], g_ref[...]).astype(BF16)
    A, M = ATTN_WIDTH, MLSTM_WIDTH

    def mm(c0, c1):
        return _dot(xn, w_ref[:, c0:c1])

    aqkv_ref[:, 0:A] = (mm(0, A) * (HEAD_DIM_ATTN ** -0.5)).astype(BF16)
    aqkv_ref[:, A:2 * A] = mm(A, 2 * A).astype(BF16)
    aqkv_ref[:, 2 * A:3 * A] = mm(2 * A, 3 * A).astype(BF16)
    o = 3 * A
    mqk_ref[:, 0:M] = mm(o, o + M)
    mqk_ref[:, M:2 * M] = mm(o + M, o + 2 * M)
    mvo_ref[:, 0:M] = mm(o + 2 * M, o + 3 * M).astype(BF16)
    mvo_ref[:, M:2 * M] = mm(o + 3 * M, o + 4 * M).astype(BF16)
    gates_ref[...] = mm(o + 4 * M, o + 4 * M + GATE_PAD)


def _inproj(x2, g, w):
    T = x2.shape[0]
    tm = ROW_TILE
    ncol = w.shape[1]
    return pl.pallas_call(
        _inproj_kernel,
        grid=(T // tm,),
        in_specs=[pl.BlockSpec((tm, D_MODEL), lambda i: (i, 0)),
                  _const_spec((1, D_MODEL)),
                  _const_spec((D_MODEL, ncol))],
        out_specs=[pl.BlockSpec((tm, 3 * ATTN_WIDTH), lambda i: (i, 0)),
                   pl.BlockSpec((tm, 2 * MLSTM_WIDTH), lambda i: (i, 0)),
                   pl.BlockSpec((tm, 2 * MLSTM_WIDTH), lambda i: (i, 0)),
                   pl.BlockSpec((tm, GATE_PAD), lambda i: (i, 0))],
        out_shape=[jax.ShapeDtypeStruct((T, 3 * ATTN_WIDTH), BF16),
                   jax.ShapeDtypeStruct((T, 2 * MLSTM_WIDTH), F32),
                   jax.ShapeDtypeStruct((T, 2 * MLSTM_WIDTH), BF16),
                   jax.ShapeDtypeStruct((T, GATE_PAD), F32)],
        compiler_params=_cparams(("parallel",), 48),
        name="inproj",
    )(x2, g, w)


def _alibi_table(dilation):
    qi = jnp.arange(BAND_BLOCK)[:, None]
    kj = jnp.arange(2 * BAND_BLOCK)[None, :]
    rel = BAND_BLOCK + qi - kj
    valid = (rel >= 0) & (rel <= BAND_BLOCK)
    h = jnp.arange(1, N_ATTN_HEADS + 1, dtype=F32)
    slopes = jnp.exp2(-8.0 * h / N_ATTN_HEADS)
    dist = (rel * dilation).astype(F32)
    return jnp.where(valid[None], -slopes[:, None, None] * dist[None], NEG_INF)


def _attn_kernel(qkv_ref, tab_ref, o_ref, lse_ref, *, dilation):
    S = qkv_ref.shape[1]
    U = S // dilation
    nb = U // BAND_BLOCK
    A = ATTN_WIDTH
    lane = lax.broadcasted_iota(jnp.int32, (BAND_BLOCK, 128), 1)
    lo = lane < HEAD_DIM_ATTN

    def block(row0, has_prev):
        krow = pl.ds(row0 - BAND_BLOCK, 2 * BAND_BLOCK) if has_prev else pl.ds(row0, BAND_BLOCK)
        qrow = pl.ds(row0, BAND_BLOCK)
        for p in range(N_ATTN_HEADS // 2):
            c = 128 * p
            q2 = qkv_ref[0, qrow, c:c + 128]
            k2 = qkv_ref[0, krow, A + c:A + c + 128]
            v2 = qkv_ref[0, krow, 2 * A + c:2 * A + c + 128]
            outs, lses = [], []
            for e in range(2):
                qm = jnp.where(lo if e == 0 else jnp.logical_not(lo), q2, jnp.zeros_like(q2))
                s = _dot_nt(qm, k2)
                if has_prev:
                    s = s + tab_ref[2 * p + e]
                else:
                    s = s + tab_ref[2 * p + e, :, BAND_BLOCK:2 * BAND_BLOCK]
                m = jnp.max(s, axis=-1, keepdims=True)
                pe = jnp.exp(s - m)
                l = jnp.sum(pe, axis=-1, keepdims=True)
                pv = _dot(pe.astype(BF16), v2)
                outs.append(pv / l)
                lses.append(m + jnp.log(l))
            o_ref[0, qrow, c:c + 128] = jnp.where(lo, outs[0], outs[1]).astype(BF16)
            lse_ref[0, qrow, c:c + 128] = jnp.where(lo, lses[0], lses[1])

    def per_residue(r, carry):
        base = pl.multiple_of(r * U, BAND_BLOCK)
        block(base, False)
        if nb > 1:
            def per_block(n, c2):
                block(pl.multiple_of(base + n * BAND_BLOCK, BAND_BLOCK), True)
                return c2
            lax.fori_loop(1, nb, per_block, 0)
        return carry

    lax.fori_loop(0, dilation, per_residue, 0)


def _attn_branch(qkv_sub, dilation):
    B, S, W = qkv_sub.shape
    tab = _alibi_table(dilation)
    return pl.pallas_call(
        functools.partial(_attn_kernel, dilation=dilation),
        grid=(B,),
        in_specs=[pl.BlockSpec((1, S, W), lambda b: (b, 0, 0)),
                  _const_spec(tab.shape)],
        out_specs=[pl.BlockSpec((1, S, ATTN_WIDTH), lambda b: (b, 0, 0)),
                   pl.BlockSpec((1, S, ATTN_WIDTH), lambda b: (b, 0, 0))],
        out_shape=[jax.ShapeDtypeStruct((B, S, ATTN_WIDTH), BF16),
                   jax.ShapeDtypeStruct((B, S, ATTN_WIDTH), F32)],
        compiler_params=_cparams(("parallel",), 48),
        name=f"attn_d{dilation}",
    )(qkv_sub, tab)


def _to_sub(t, d):
    B, S, W = t.shape
    if d == 1:
        return t
    return t.reshape(B, S // d, d, W).transpose(0, 2, 1, 3).reshape(B, S, W)


def _from_sub(t, d):
    B, S, W = t.shape
    if d == 1:
        return t
    return t.reshape(B, d, S // d, W).transpose(0, 2, 1, 3).reshape(B, S, W)


def _log_sigmoid(x):
    return jnp.minimum(x, 0.0) - jnp.log(1.0 + jnp.exp(-jnp.abs(x)))


def _sigmoid(x):
    return 1.0 / (1.0 + jnp.exp(-x))


def _mlstm_kernel(q_ref, k_ref, v_ref, og_ref, gc_ref, gr_ref, gbc_ref, gbr_ref,
                  cwq_ref, cwk_ref, cbq_ref, cbk_ref, hg_ref, out_ref, qs_ref, ks_ref):
    h = pl.program_id(1)
    S = q_ref.shape[1]
    L = MLSTM_CHUNK
    dh = HEAD_DIM_MLSTM
    row = lax.broadcasted_iota(jnp.int32, (S, dh), 0)

    def conv_silu(x, w_ref, b_ref):
        acc = b_ref[...] + w_ref[0:1, :] * x
        for j in range(1, MLSTM_CONV):
            acc = acc + w_ref[j:j + 1, :] * jnp.where(row >= j, pltpu.roll(x, j, 0), 0.0)
        return acc * _sigmoid(acc)

    qs_ref[...] = conv_silu(q_ref[0], cwq_ref, cbq_ref).astype(BF16)
    ks_ref[...] = (conv_silu(k_ref[0], cwk_ref, cbk_ref) * (dh ** -0.5)).astype(BF16)

    lane = lax.broadcasted_iota(jnp.int32, (L, GATE_PAD), 1)
    ti = lax.broadcasted_iota(jnp.int32, (L, L), 0)
    si = lax.broadcasted_iota(jnp.int32, (L, L), 1)
    causal = si <= ti

    def chunk(c, carry):
        C, n, m = carry
        r0 = pl.multiple_of(c * L, L)
        rows = pl.ds(r0, L)
        qc = qs_ref[rows, :]
        kc = ks_ref[rows, :]
        vc = v_ref[0, rows, :]
        gfull = gc_ref[0, rows, :] + gbc_ref[...]
        li_col = jnp.sum(jnp.where(lane == h, gfull, 0.0), axis=-1, keepdims=True)
        lf_col = _log_sigmoid(jnp.sum(jnp.where(lane == h + N_MLSTM_HEADS, gfull, 0.0),
                                      axis=-1, keepdims=True))
        li_row = gr_ref[0, pl.ds(c, 1), :] + gbr_ref[0]
        lf_row = _log_sigmoid(gr_ref[1, pl.ds(c, 1), :] + gbr_ref[1])
        b_col = jnp.sum(jnp.where(causal, lf_row, 0.0), axis=-1, keepdims=True)
        b_row = jnp.sum(jnp.where(ti <= si, lf_col, 0.0), axis=0, keepdims=True)
        g = jnp.sum(lf_row, axis=-1, keepdims=True)

        D = jnp.where(causal, b_col - b_row + li_row, NEG_INF)
        inter_log = b_col + m
        m_t = jnp.maximum(jnp.max(D, axis=-1, keepdims=True), inter_log)
        Sm = jnp.exp(D - m_t) * _dot_nt(qc, kc)
        inter_w = jnp.exp(inter_log - m_t)
        num = inter_w * _dot(qc, C.astype(BF16)) + _dot(Sm.astype(BF16), vc)
        qn = jnp.sum(qc.astype(F32) * n, axis=-1, keepdims=True)
        den = inter_w * qn + jnp.sum(Sm, axis=-1, keepdims=True)
        hh = num / jnp.maximum(jnp.abs(den), jnp.exp(-m_t))
        hn = hh * lax.rsqrt(jnp.mean(hh * hh, axis=-1, keepdims=True) + RMS_EPS) * hg_ref[...]
        out_ref[0, rows, :] = (hn * _sigmoid(og_ref[0, rows, :].astype(F32))).astype(BF16)

        a_col = g - b_col + li_col
        m_loc = jnp.max(a_col, axis=0, keepdims=True)
        kw = kc.astype(F32) * jnp.exp(a_col - m_loc)
        kv = _dot(kw.T.astype(BF16), vc)
        ksum = jnp.sum(kw, axis=0, keepdims=True)
        m_new = jnp.maximum(g + m, m_loc)
        dec = jnp.exp(g + m - m_new)
        inj = jnp.exp(m_loc - m_new)
        return dec * C + inj * kv, dec * n + inj * ksum, m_new

    init = (jnp.zeros((dh, dh), F32), jnp.zeros((1, dh), F32), jnp.zeros((1, 1), F32))
    lax.fori_loop(0, S // L, chunk, init)


def _mlstm(mqk, mvo, gates, gate_b, conv_w, conv_b, head_g):
    B, S, _ = mqk.shape
    H, dh, L = N_MLSTM_HEADS, HEAD_DIM_MLSTM, MLSTM_CHUNK
    nc = S // L
    gates_row = gates[:, :, :2 * H].transpose(0, 2, 1).reshape(B, 2, H, nc, L)
    gb_col = jnp.pad(gate_b.reshape(1, 2 * H), ((0, 0), (0, GATE_PAD - 2 * H)))
    gb_row = gate_b.reshape(2, H, 1, 1)
    cb = conv_b.reshape(1, 2 * MLSTM_WIDTH)
    hg = head_g.reshape(1, MLSTM_WIDTH)
    seq = lambda off: pl.BlockSpec((1, S, dh), lambda b, h: (b, 0, h + off))
    return pl.pallas_call(
        _mlstm_kernel,
        grid=(B, H),
        in_specs=[seq(0), seq(H), seq(0), seq(H),
                  pl.BlockSpec((1, S, GATE_PAD), lambda b, h: (b, 0, 0)),
                  pl.BlockSpec((None, 2, None, nc, L), lambda b, h: (b, 0, h, 0, 0)),
                  _const_spec((1, GATE_PAD)),
                  pl.BlockSpec((2, None, 1, 1), lambda b, h: (0, h, 0, 0)),
                  pl.BlockSpec((MLSTM_CONV, dh), lambda b, h: (0, h)),
                  pl.BlockSpec((MLSTM_CONV, dh), lambda b, h: (0, h + H)),
                  pl.BlockSpec((1, dh), lambda b, h: (0, h)),
                  pl.BlockSpec((1, dh), lambda b, h: (0, h + H)),
                  pl.BlockSpec((1, dh), lambda b, h: (0, h))],
        out_specs=pl.BlockSpec((1, S, dh), lambda b, h: (b, 0, h)),
        out_shape=jax.ShapeDtypeStruct((B, S, MLSTM_WIDTH), BF16),
        scratch_shapes=[pltpu.VMEM((S, dh), BF16), pltpu.VMEM((S, dh), BF16)],
        compiler_params=_cparams(("parallel", "parallel"), 32),
        name="mlstm",
    )(mqk, mqk, mvo, mvo, gates, gates_row, gb_col, gb_row, conv_w, conv_w, cb, cb, hg)


def _outproj_kernel(x_ref, o1_ref, o2_ref, o3_ref, l1_ref, l2_ref, l3_ref, hm_ref, w_ref, out_ref):
    l1, l2, l3 = l1_ref[...], l2_ref[...], l3_ref[...]
    mx = jnp.maximum(jnp.maximum(l1, l2), l3)
    e1, e2, e3 = jnp.exp(l1 - mx), jnp.exp(l2 - mx), jnp.exp(l3 - mx)
    attn = (e1 * o1_ref[...].astype(F32) + e2 * o2_ref[...].astype(F32)
            + e3 * o3_ref[...].astype(F32)) / (e1 + e2 + e3)
    A = ATTN_WIDTH
    y = _dot(attn.astype(BF16), w_ref[0:A, :]) + _dot(hm_ref[...], w_ref[A:, :])
    out_ref[...] = x_ref[...] + y


def _outproj(x2, os_, ls_, hm2, w):
    T = x2.shape[0]
    tm = ROW_TILE
    half = pl.BlockSpec((tm, ATTN_WIDTH), lambda i: (i, 0))
    full = pl.BlockSpec((tm, D_MODEL), lambda i: (i, 0))
    return pl.pallas_call(
        _outproj_kernel,
        grid=(T // tm,),
        in_specs=[full, half, half, half, half, half, half, half, _const_spec((D_MODEL, D_MODEL))],
        out_specs=full,
        out_shape=jax.ShapeDtypeStruct((T, D_MODEL), F32),
        compiler_params=_cparams(("parallel",), 40),
        name="outproj",
    )(x2, *os_, *ls_, hm2, w)


def _memkv_kernel(mem_ref, g_ref, w_ref, k_ref, v_ref):
    mn = _rms(mem_ref[0], g_ref[...]).astype(BF16)
    k_ref[0] = _dot(mn, w_ref[:, 0:D_MODEL]).astype(BF16)
    v_ref[0] = _dot(mn, w_ref[:, D_MODEL:]).astype(BF16)


def _memkv(mem, g, w):
    B, M, _ = mem.shape
    blk = pl.BlockSpec((1, M, D_MODEL), lambda b: (b, 0, 0))
    return pl.pallas_call(
        _memkv_kernel,
        grid=(B,),
        in_specs=[blk, _const_spec((1, D_MODEL)), _const_spec((D_MODEL, 2 * D_MODEL))],
        out_specs=[blk, blk],
        out_shape=[jax.ShapeDtypeStruct((B, M, D_MODEL), BF16)] * 2,
        compiler_params=_cparams(("parallel",), 32),
        name="memkv",
    )(mem, g, w)


def _xattn_kernel(h_ref, g_ref, wq_ref, k_ref, v_ref, wo_ref, out_ref, o_scr):
    hres = h_ref[0]
    hn = _rms(hres, g_ref[...]).astype(BF16)
    q = (_dot(hn, wq_ref[...]) * (HEAD_DIM_XATTN ** -0.5)).astype(BF16)
    dh = HEAD_DIM_XATTN
    for hd in range(N_XATTN_HEADS):
        cols = slice(hd * dh, (hd + 1) * dh)
        s = _dot_nt(q[:, cols], k_ref[0, :, cols])
        m = jnp.max(s, axis=-1, keepdims=True)
        pe = jnp.exp(s - m)
        l = jnp.sum(pe, axis=-1, keepdims=True)
        o_scr[:, cols] = (_dot(pe.astype(BF16), v_ref[0, :, cols]) / l).astype(BF16)
    out_ref[0] = hres + _dot(o_scr[...], wo_ref[...])


def _xattn(h, g, wq, k, v, wo):
    B, S, _ = h.shape
    M = k.shape[1]
    tm = ROW_TILE
    rows = pl.BlockSpec((1, tm, D_MODEL), lambda b, i: (b, i, 0))
    kvb = pl.BlockSpec((1, M, D_MODEL), lambda b, i: (b, 0, 0))
    sq = _const_spec((D_MODEL, D_MODEL))
    return pl.pallas_call(
        _xattn_kernel,
        grid=(B, S // tm),
        in_specs=[rows, _const_spec((1, D_MODEL)), sq, kvb, kvb, sq],
        out_specs=rows,
        out_shape=jax.ShapeDtypeStruct((B, S, D_MODEL), F32),
        scratch_shapes=[pltpu.VMEM((tm, D_MODEL), BF16)],
        compiler_params=_cparams(("parallel", "parallel"), 40),
        name="xattn",
    )(h, g, wq, k, v, wo)


def _ffn_kernel(h_ref, g_ref, wg_ref, wu_ref, cw_ref, cb_ref, wd_ref, gf_ref, out_ref, hn_ref,
                *, final_norm):
    j = pl.program_id(1)
    S = h_ref.shape[1]
    R = FFN_ROWS
    FC = FFN_CHUNK

    @pl.when(j == 0)
    def _():
        hres = h_ref[0]
        hn_ref[...] = _rms(hres, g_ref[...]).astype(BF16)
        out_ref[0] = hres

    row = lax.broadcasted_iota(jnp.int32, (R, FC), 0)
    w0, w1, w2 = cw_ref[0:1, :], cw_ref[1:2, :], cw_ref[2:3, :]

    def sub(i, tail):
        rows = pl.ds(pl.multiple_of(i * R, R), R)
        hn = hn_ref[rows, :]
        gate = _dot(hn, wg_ref[...])
        up = _dot(hn, wu_ref[...])
        t1 = tail[7:8, :]
        t2 = tail[6:7, :]
        g1 = jnp.where(row == 0, t1, pltpu.roll(gate, 1, 0))
        g2 = jnp.where(row == 0, t2, jnp.where(row == 1, t1, pltpu.roll(gate, 2, 0)))
        cv = cb_ref[...] + w0 * gate + w1 * g1 + w2 * g2
        act = (cv * _sigmoid(cv) * up).astype(BF16)
        out_ref[0, rows, :] += _dot(act, wd_ref[...])
        return gate[R - 8:R, :]

    lax.fori_loop(0, S // R, sub, jnp.zeros((8, FC), F32))

    if final_norm:
        @pl.when(j == pl.num_programs(1) - 1)
        def _():
            out_ref[0] = _rms(out_ref[0], gf_ref[...])


def _ffn(h, g, w_up, conv_w, conv_b, w_down, g_final, final_norm):
    B, S, _ = h.shape
    FC = FFN_CHUNK
    nchunk = D_FF // FC
    seq = pl.BlockSpec((1, S, D_MODEL), lambda b, j: (b, 0, 0))
    return pl.pallas_call(
        functools.partial(_ffn_kernel, final_norm=final_norm),
        grid=(B, nchunk),
        in_specs=[seq, _const_spec((1, D_MODEL)),
                  pl.BlockSpec((D_MODEL, FC), lambda b, j: (0, j)),
                  pl.BlockSpec((D_MODEL, FC), lambda b, j: (0, j + nchunk)),
                  pl.BlockSpec((FFN_CONV, FC), lambda b, j: (0, j)),
                  pl.BlockSpec((1, FC), lambda b, j: (0, j)),
                  pl.BlockSpec((FC, D_MODEL), lambda b, j: (j, 0)),
                  _const_spec((1, D_MODEL))],
        out_specs=seq,
        out_shape=jax.ShapeDtypeStruct((B, S, D_MODEL), F32),
        scratch_shapes=[pltpu.VMEM((S, D_MODEL), BF16)],
        compiler_params=_cparams(("parallel", "arbitrary"), 52),
        name="ffn",
    )(h, g, w_up, w_up, conv_w, conv_b, w_down, g_final)


def kernel(x, mem, norm_mix_g, w_in, mlstm_conv_w, mlstm_conv_b, mlstm_gate_b, mlstm_head_g,
           w_mix_out, norm_xattn_g, norm_mem_g, w_xq, w_xkv, w_xo, norm_ffn_g, w_ffn_up,
           ffn_conv_w, ffn_conv_b, w_ffn_down, norm_final_g):
    B, S, D = x.shape
    depth = w_in.shape[0]
    row1 = lambda v: v.reshape(1, -1)
    h = x
    for l in range(depth):
        w_in_p = jnp.pad(w_in[l], ((0, 0), (0, GATE_PAD - 2 * N_MLSTM_HEADS))).astype(BF16)
        aqkv, mqk, mvo, gates = _inproj(h.reshape(B * S, D), row1(norm_mix_g[l]), w_in_p)
        aqkv = aqkv.reshape(B, S, -1)
        outs, lses = [], []
        for _, dil in DILATED_BRANCHES:
            o, lse = _attn_branch(_to_sub(aqkv, dil), dil)
            outs.append(_from_sub(o, dil).reshape(B * S, -1))
            lses.append(_from_sub(lse, dil).reshape(B * S, -1))
        hm = _mlstm(mqk.reshape(B, S, -1), mvo.reshape(B, S, -1), gates.reshape(B, S, -1),
                    mlstm_gate_b[l], mlstm_conv_w[l], mlstm_conv_b[l], mlstm_head_g[l])
        h = _outproj(h.reshape(B * S, D), outs, lses, hm.reshape(B * S, -1),
                     w_mix_out[l].astype(BF16)).reshape(B, S, D)
        k, v = _memkv(mem, row1(norm_mem_g[l]), w_xkv[l].astype(BF16))
        h = _xattn(h, row1(norm_xattn_g[l]), w_xq[l].astype(BF16), k, v, w_xo[l].astype(BF16))
        h = _ffn(h, row1(norm_ffn_g[l]), w_ffn_up[l].astype(BF16), ffn_conv_w[l],
                 row1(ffn_conv_b[l]), w_ffn_down[l].astype(BF16), row1(norm_final_g),
                 final_norm=l == depth - 1)
    return h
```

```python
import functools

import jax
import jax.numpy as jnp
from jax import lax
from jax.experimental import pallas as pl
from jax.experimental.pallas import tpu as pltpu

F32 = jnp.float32
BF16 = jnp.bfloat16

D_MODEL = 1024
ATTN_WIDTH = 512
HEAD_DIM_ATTN = 64
N_ATTN_HEADS = 8
DILATIONS = (1, 4, 16)
BAND_BLOCK = 128
MLSTM_WIDTH = 512
N_MLSTM_HEADS = 4
HEAD_DIM_MLSTM = 128
MLSTM_CONV = 4
N_XATTN_HEADS = 4
HEAD_DIM_XATTN = 256
D_FF = 2816
FFN_CONV = 3
RMS_EPS = 1e-6
NEG_INF = -1e30

LANES = 128
SUBLANES = 8
GATE_PAD = LANES
MLSTM_CHUNK = 256
ROW_TILE = 512
FFN_CHUNK = 256
FFN_ROWS = 256
MIB = 1024 * 1024


def _cparams(semantics, vmem_mib):
    return pltpu.CompilerParams(dimension_semantics=semantics, vmem_limit_bytes=vmem_mib * MIB)


def _rms(x, g):
    return x * lax.rsqrt(jnp.mean(x * x, axis=-1, keepdims=True) + RMS_EPS) * g


def _dot(a, b):
    return jnp.dot(a, b, preferred_element_type=F32)


def _dot_nt(a, b):
    return lax.dot_general(a, b, (((1,), (1,)), ((), ())), preferred_element_type=F32)


def _sigmoid(x):
    return 1.0 / (1.0 + jnp.exp(-x))


def _log_sigmoid(x):
    return jnp.minimum(x, 0.0) - jnp.log(1.0 + jnp.exp(-jnp.abs(x)))


def _const_spec(shape):
    return pl.BlockSpec(shape, lambda *_: (0,) * len(shape))


def _causal_taps(x, tail, n_taps):
    R = x.shape[0]
    row = lax.broadcasted_iota(jnp.int32, x.shape, 0)
    taps = [x]
    for j in range(1, n_taps):
        head = jnp.tile(pltpu.roll(tail, j, 0), (R // SUBLANES, 1))
        taps.append(jnp.where(row < j, head, pltpu.roll(x, j, 0)))
    return taps


def _inproj_kernel(x_ref, g_ref, w_ref, aqkv_ref, mqk_ref, mvo_ref, gates_ref):
    xn = _rms(x_ref[...], g_ref[...]).astype(BF16)
    A, M = ATTN_WIDTH, MLSTM_WIDTH

    def mm(c0, c1):
        return _dot(xn, w_ref[:, c0:c1])

    aqkv_ref[:, 0:A] = mm(0, A) * (HEAD_DIM_ATTN ** -0.5)
    aqkv_ref[:, A:2 * A] = mm(A, 2 * A)
    aqkv_ref[:, 2 * A:3 * A] = mm(2 * A, 3 * A)
    o = 3 * A
    mqk_ref[:, 0:M] = mm(o, o + M)
    mqk_ref[:, M:2 * M] = mm(o + M, o + 2 * M)
    mvo_ref[:, 0:M] = mm(o + 2 * M, o + 3 * M).astype(BF16)
    mvo_ref[:, M:2 * M] = mm(o + 3 * M, o + 4 * M).astype(BF16)
    gates_ref[...] = mm(o + 4 * M, o + 4 * M + GATE_PAD)


def _inproj(x2, g, w):
    T = x2.shape[0]
    tm = ROW_TILE
    ncol = w.shape[1]
    return pl.pallas_call(
        _inproj_kernel,
        grid=(T // tm,),
        in_specs=[pl.BlockSpec((tm, D_MODEL), lambda i: (i, 0)),
                  _const_spec((1, D_MODEL)),
                  _const_spec((D_MODEL, ncol))],
        out_specs=[pl.BlockSpec((tm, 3 * ATTN_WIDTH), lambda i: (i, 0)),
                   pl.BlockSpec((tm, 2 * MLSTM_WIDTH), lambda i: (i, 0)),
                   pl.BlockSpec((tm, 2 * MLSTM_WIDTH), lambda i: (i, 0)),
                   pl.BlockSpec((tm, GATE_PAD), lambda i: (i, 0))],
        out_shape=[jax.ShapeDtypeStruct((T, 3 * ATTN_WIDTH), F32),
                   jax.ShapeDtypeStruct((T, 2 * MLSTM_WIDTH), F32),
                   jax.ShapeDtypeStruct((T, 2 * MLSTM_WIDTH), BF16),
                   jax.ShapeDtypeStruct((T, GATE_PAD), F32)],
        compiler_params=_cparams(("parallel",), 48),
        name="inproj",
    )(x2, g, w)


def _alibi_tables():
    qi = jnp.arange(BAND_BLOCK)[:, None]
    kj = jnp.arange(2 * BAND_BLOCK)[None, :]
    rel = BAND_BLOCK + qi - kj
    valid = (rel >= 0) & (rel <= BAND_BLOCK)
    h = jnp.arange(1, N_ATTN_HEADS + 1, dtype=F32)
    slopes = jnp.exp2(-8.0 * h / N_ATTN_HEADS)
    tabs = []
    for d in DILATIONS:
        dist = (rel * d).astype(F32)
        tabs.append(jnp.where(valid[None], -slopes[:, None, None] * dist[None], NEG_INF))
    return jnp.stack(tabs)


def _attn_kernel(q_ref, k_ref, v_ref, tab_ref, out_ref,
                 qlo_s, qhi_s, k_s, v_s, acc_s, m_s, l_s):
    S = q_ref.shape[1]
    B = BAND_BLOCK
    lane1 = lax.broadcasted_iota(jnp.int32, (1, LANES), 1)
    mlo = (lane1 < HEAD_DIM_ATTN).astype(F32)
    mhi = 1.0 - mlo
    lo = lax.broadcasted_iota(jnp.int32, (B, LANES), 1) < HEAD_DIM_ATTN

    for bi, d in enumerate(DILATIONS):
        U = S // d
        for r in range(d):
            src = pl.ds(r, U, stride=d) if d > 1 else pl.ds(0, S)
            dst = pl.ds(r * U, U)
            qf = q_ref[0, src, :]
            qlo_s[bi, dst, :] = (qf * mlo).astype(BF16)
            qhi_s[bi, dst, :] = (qf * mhi).astype(BF16)
            k_s[bi, dst, :] = k_ref[0, src, :].astype(BF16)
            v_s[bi, dst, :] = v_ref[0, src, :].astype(BF16)

    def group(bi, row0s, starts, has_prev):
        d = DILATIONS[bi]
        scores = []
        for row0 in row0s:
            qrow = pl.ds(row0, B)
            krow = pl.ds(row0 - B, 2 * B) if has_prev else qrow
            k2 = k_s[bi, krow, :]
            v2 = v_s[bi, krow, :]
            scores.append((_dot_nt(qlo_s[bi, qrow, :], k2), v2, 0))
            scores.append((_dot_nt(qhi_s[bi, qrow, :], k2), v2, 1))
        soft = []
        for s, v2, e in scores:
            t = tab_ref[bi, e] if has_prev else tab_ref[bi, e, :, B:2 * B]
            s = s + t
            m = jnp.max(s, axis=-1, keepdims=True)
            pe = jnp.exp(s - m)
            soft.append((m, jnp.sum(pe, axis=-1, keepdims=True), pe.astype(BF16), v2))
        pvs = [_dot(pe, v2) for _, _, pe, v2 in soft]
        for i, start in enumerate(starts):
            m = jnp.where(lo, soft[2 * i][0], soft[2 * i + 1][0])
            l = jnp.where(lo, soft[2 * i][1], soft[2 * i + 1][1])
            pv = jnp.where(lo, pvs[2 * i], pvs[2 * i + 1])
            pos = pl.ds(start, B, stride=d) if d > 1 else pl.ds(start, B)
            if bi == 0:
                m_s[pos, :] = m
                l_s[pos, :] = l
                acc_s[pos, :] = pv
            else:
                m_old = m_s[pos, :]
                m_new = jnp.maximum(m_old, m)
                a_old = jnp.exp(m_old - m_new)
                a_blk = jnp.exp(m - m_new)
                m_s[pos, :] = m_new
                l_s[pos, :] = a_old * l_s[pos, :] + a_blk * l
                acc_s[pos, :] = a_old * acc_s[pos, :] + a_blk * pv

    al = lambda v: pl.multiple_of(v, B)

    group(0, [0], [0], False)

    def d1_body(g, c):
        offs = [al((1 + 5 * g + i) * B) for i in range(5)]
        group(0, offs, offs, True)
        return c
    lax.fori_loop(0, 3, d1_body, 0)

    U4 = S // 4
    group(1, [r * U4 for r in range(4)], list(range(4)), False)

    def d4_body(n, c):
        group(1, [al(r * U4 + n * B) for r in range(4)], [r + n * (4 * B) for r in range(4)], True)
        return c
    lax.fori_loop(1, U4 // B, d4_body, 0)

    def d16_body(g, c):
        group(2, [al((4 * g + i) * B) for i in range(4)], [4 * g + i for i in range(4)], False)
        return c
    lax.fori_loop(0, 4, d16_body, 0)

    out_ref[0] = (acc_s[...] / l_s[...]).astype(BF16)


def _attn(aqkv):
    B, S, _ = aqkv.shape
    npair = N_ATTN_HEADS // 2
    assert S == 16 * BAND_BLOCK, "block schedule below is written for 16 blocks of 128"
    tabs = _alibi_tables()
    seq = lambda off: pl.BlockSpec((1, S, LANES), lambda b, p: (b, 0, p + off))
    return pl.pallas_call(
        _attn_kernel,
        grid=(B, npair),
        in_specs=[seq(0), seq(npair), seq(2 * npair),
                  pl.BlockSpec((len(DILATIONS), 2, BAND_BLOCK, 2 * BAND_BLOCK), lambda b, p: (0, p, 0, 0))],
        out_specs=seq(0),
        out_shape=jax.ShapeDtypeStruct((B, S, ATTN_WIDTH), BF16),
        scratch_shapes=[pltpu.VMEM((len(DILATIONS), S, LANES), BF16)] * 4
                       + [pltpu.VMEM((S, LANES), F32)] * 3,
        compiler_params=_cparams(("parallel", "parallel"), 32),
        name="attn",
    )(aqkv, aqkv, aqkv, tabs)


def _mlstm_kernel(qk_ref, vo_ref, gc_ref, gr_ref, gbc_ref, gbr_ref, cw_ref, cb_ref, hg_ref, out_ref):
    S = qk_ref.shape[1]
    L = MLSTM_CHUNK
    H, dh, W = N_MLSTM_HEADS, HEAD_DIM_MLSTM, MLSTM_WIDTH
    ti = lax.broadcasted_iota(jnp.int32, (L, L), 0)
    si = lax.broadcasted_iota(jnp.int32, (L, L), 1)
    causal = si <= ti

    def conv_silu(taps, col):
        acc = cb_ref[:, col]
        for j, t in enumerate(taps):
            acc = acc + cw_ref[j:j + 1, col] * t
        return acc * _sigmoid(acc)

    def chunk(c, carry):
        r0 = pl.multiple_of(c * L, L)
        rows = pl.ds(r0, L)
        gfull = gc_ref[0, rows, :] + gbc_ref[...]
        prep, new_carry = [], []
        for h in range(H):
            C, n, m, qtail, ktail = carry[h]
            qcol = slice(h * dh, (h + 1) * dh)
            kcol = slice(W + h * dh, W + (h + 1) * dh)
            qraw = qk_ref[0, rows, qcol]
            kraw = qk_ref[0, rows, kcol]
            qc = conv_silu(_causal_taps(qraw, qtail, MLSTM_CONV), qcol).astype(BF16)
            kc = (conv_silu(_causal_taps(kraw, ktail, MLSTM_CONV), kcol) * (dh ** -0.5)).astype(BF16)
            vc = vo_ref[0, rows, qcol]
            li_col = gfull[:, h:h + 1]
            lf_col = _log_sigmoid(gfull[:, H + h:H + h + 1])
            li_row = gr_ref[0, h, pl.ds(c, 1), :] + gbr_ref[0, h]
            lf_row = _log_sigmoid(gr_ref[1, h, pl.ds(c, 1), :] + gbr_ref[1, h])
            b_col = jnp.sum(jnp.where(causal, lf_row, 0.0), axis=-1, keepdims=True)
            b_row = jnp.sum(jnp.where(ti <= si, lf_col, 0.0), axis=0, keepdims=True)
            g = jnp.sum(lf_row, axis=-1, keepdims=True)
            qk = _dot_nt(qc, kc)
            qC = _dot(qc, C.astype(BF16))
            a_col = g - b_col + li_col
            m_loc = jnp.max(a_col, axis=0, keepdims=True)
            kw = kc.astype(F32) * jnp.exp(a_col - m_loc)
            kv = _dot(kw.T.astype(BF16), vc)
            prep.append((qc, vc, li_row, b_col, b_row, qk, qC, n, m))
            m_new = jnp.maximum(g + m, m_loc)
            dec = jnp.exp(g + m - m_new)
            inj = jnp.exp(m_loc - m_new)
            new_carry.append((dec * C + inj * kv,
                              dec * n + inj * jnp.sum(kw, axis=0, keepdims=True),
                              m_new, qraw[L - SUBLANES:L, :], kraw[L - SUBLANES:L, :]))
        mid = []
        for qc, vc, li_row, b_col, b_row, qk, qC, n, m in prep:
            D = jnp.where(causal, b_col - b_row + li_row, NEG_INF)
            inter_log = b_col + m
            m_t = jnp.maximum(jnp.max(D, axis=-1, keepdims=True), inter_log)
            Sm = jnp.exp(D - m_t) * qk
            inter_w = jnp.exp(inter_log - m_t)
            qn = jnp.sum(qc.astype(F32) * n, axis=-1, keepdims=True)
            den = inter_w * qn + jnp.sum(Sm, axis=-1, keepdims=True)
            mid.append((inter_w * qC, _dot(Sm.astype(BF16), vc), den, m_t))
        for h, (inter, intra, den, m_t) in enumerate(mid):
            col = slice(h * dh, (h + 1) * dh)
            hh = (inter + intra) / jnp.maximum(jnp.abs(den), jnp.exp(-m_t))
            hn = hh * lax.rsqrt(jnp.mean(hh * hh, axis=-1, keepdims=True) + RMS_EPS) * hg_ref[:, col]
            og = vo_ref[0, rows, W + h * dh:W + (h + 1) * dh].astype(F32)
            out_ref[0, rows, col] = (hn * _sigmoid(og)).astype(BF16)
        return tuple(new_carry)

    z = lambda *s: jnp.zeros(s, F32)
    init = tuple((z(dh, dh), z(1, dh), z(1, 1), z(SUBLANES, dh), z(SUBLANES, dh)) for _ in range(H))
    lax.fori_loop(0, S // L, chunk, init)


def _mlstm(mqk, mvo, gates, gate_b, conv_w, conv_b, head_g):
    B, S, _ = mqk.shape
    H, L, W = N_MLSTM_HEADS, MLSTM_CHUNK, MLSTM_WIDTH
    nc = S // L
    gates_row = gates[:, :, :2 * H].transpose(0, 2, 1).reshape(B, 2, H, nc, L)
    gb_col = jnp.pad(gate_b.reshape(1, 2 * H), ((0, 0), (0, GATE_PAD - 2 * H)))
    gb_row = gate_b.reshape(2, H, 1, 1)
    seq = lambda w: pl.BlockSpec((1, S, w), lambda b: (b, 0, 0))
    return pl.pallas_call(
        _mlstm_kernel,
        grid=(B,),
        in_specs=[seq(2 * W), seq(2 * W), seq(GATE_PAD),
                  pl.BlockSpec((None, 2, H, nc, L), lambda b: (b, 0, 0, 0, 0)),
                  _const_spec((1, GATE_PAD)), _const_spec((2, H, 1, 1)),
                  _const_spec((MLSTM_CONV, 2 * W)), _const_spec((1, 2 * W)), _const_spec((1, W))],
        out_specs=seq(W),
        out_shape=jax.ShapeDtypeStruct((B, S, W), BF16),
        compiler_params=_cparams(("parallel",), 48),
        name="mlstm",
    )(mqk, mvo, gates, gates_row, gb_col, gb_row, conv_w, conv_b.reshape(1, 2 * W), head_g.reshape(1, W))


def _outproj_kernel(x_ref, attn_ref, hm_ref, w_ref, out_ref):
    A = ATTN_WIDTH
    out_ref[...] = x_ref[...] + _dot(attn_ref[...], w_ref[0:A, :]) + _dot(hm_ref[...], w_ref[A:, :])


def _outproj(x2, attn2, hm2, w):
    T = x2.shape[0]
    tm = ROW_TILE
    half = pl.BlockSpec((tm, ATTN_WIDTH), lambda i: (i, 0))
    full = pl.BlockSpec((tm, D_MODEL), lambda i: (i, 0))
    return pl.pallas_call(
        _outproj_kernel,
        grid=(T // tm,),
        in_specs=[full, half, half, _const_spec((D_MODEL, D_MODEL))],
        out_specs=full,
        out_shape=jax.ShapeDtypeStruct((T, D_MODEL), F32),
        compiler_params=_cparams(("parallel",), 32),
        name="outproj",
    )(x2, attn2, hm2, w)


def _memkv_kernel(mem_ref, g_ref, w_ref, k_ref, v_ref):
    mn = _rms(mem_ref[0], g_ref[...]).astype(BF16)
    k_ref[0] = _dot(mn, w_ref[:, 0:D_MODEL]).astype(BF16)
    v_ref[0] = _dot(mn, w_ref[:, D_MODEL:]).astype(BF16)


def _memkv(mem, g, w):
    B, M, _ = mem.shape
    blk = pl.BlockSpec((1, M, D_MODEL), lambda b: (b, 0, 0))
    return pl.pallas_call(
        _memkv_kernel,
        grid=(B,),
        in_specs=[blk, _const_spec((1, D_MODEL)), _const_spec((D_MODEL, 2 * D_MODEL))],
        out_specs=[blk, blk],
        out_shape=[jax.ShapeDtypeStruct((B, M, D_MODEL), BF16)] * 2,
        compiler_params=_cparams(("parallel",), 32),
        name="memkv",
    )(mem, g, w)


def _xattn_kernel(h_ref, g_ref, wq_ref, k_ref, v_ref, wo_ref, out_ref, o_scr):
    hres = h_ref[0]
    hn = _rms(hres, g_ref[...]).astype(BF16)
    q = (_dot(hn, wq_ref[...]) * (HEAD_DIM_XATTN ** -0.5)).astype(BF16)
    dh = HEAD_DIM_XATTN
    cols = [slice(hd * dh, (hd + 1) * dh) for hd in range(N_XATTN_HEADS)]
    scores = [_dot_nt(q[:, c], k_ref[0, :, c]) for c in cols]
    soft = []
    for s in scores:
        m = jnp.max(s, axis=-1, keepdims=True)
        pe = jnp.exp(s - m)
        soft.append((pe.astype(BF16), jnp.sum(pe, axis=-1, keepdims=True)))
    pvs = [_dot(pe, v_ref[0, :, c]) for (pe, _), c in zip(soft, cols)]
    for pv, (_, l), c in zip(pvs, soft, cols):
        o_scr[:, c] = (pv / l).astype(BF16)
    out_ref[0] = hres + _dot(o_scr[...], wo_ref[...])


def _xattn(h, g, wq, k, v, wo):
    B, S, _ = h.shape
    M = k.shape[1]
    tm = ROW_TILE
    rows = pl.BlockSpec((1, tm, D_MODEL), lambda b, i: (b, i, 0))
    kvb = pl.BlockSpec((1, M, D_MODEL), lambda b, i: (b, 0, 0))
    sq = _const_spec((D_MODEL, D_MODEL))
    return pl.pallas_call(
        _xattn_kernel,
        grid=(B, S // tm),
        in_specs=[rows, _const_spec((1, D_MODEL)), sq, kvb, kvb, sq],
        out_specs=rows,
        out_shape=jax.ShapeDtypeStruct((B, S, D_MODEL), F32),
        scratch_shapes=[pltpu.VMEM((tm, D_MODEL), BF16)],
        compiler_params=_cparams(("parallel", "parallel"), 40),
        name="xattn",
    )(h, g, wq, k, v, wo)


def _ffn_kernel(h_ref, g_ref, wg_ref, wu_ref, cw_ref, cb_ref, wd_ref, gf_ref, out_ref, hn_ref,
                *, final_norm):
    j = pl.program_id(1)
    S = h_ref.shape[1]
    R = FFN_ROWS
    nblk = S // R

    @pl.when(j == 0)
    def _():
        hres = h_ref[0]
        hn_ref[...] = _rms(hres, g_ref[...]).astype(BF16)
        out_ref[0] = hres

    def up_dots(i):
        hn = hn_ref[i * R:(i + 1) * R, :]
        return _dot(hn, wg_ref[...]), _dot(hn, wu_ref[...])

    nxt = up_dots(0)
    tail = jnp.zeros((SUBLANES, FFN_CHUNK), F32)
    for i in range(nblk):
        gate, up = nxt
        if i + 1 < nblk:
            nxt = up_dots(i + 1)
        cv = cb_ref[...]
        for jt, t in enumerate(_causal_taps(gate, tail, FFN_CONV)):
            cv = cv + cw_ref[jt:jt + 1, :] * t
        tail = gate[R - SUBLANES:R, :]
        act = (cv * _sigmoid(cv) * up).astype(BF16)
        out_ref[0, i * R:(i + 1) * R, :] += _dot(act, wd_ref[...])

    if final_norm:
        @pl.when(j == pl.num_programs(1) - 1)
        def _():
            out_ref[0] = _rms(out_ref[0], gf_ref[...])


def _ffn(h, g, w_up, conv_w, conv_b, w_down, g_final, final_norm):
    B, S, _ = h.shape
    FC = FFN_CHUNK
    nchunk = D_FF // FC
    seq = pl.BlockSpec((1, S, D_MODEL), lambda b, j: (b, 0, 0))
    return pl.pallas_call(
        functools.partial(_ffn_kernel, final_norm=final_norm),
        grid=(B, nchunk),
        in_specs=[seq, _const_spec((1, D_MODEL)),
                  pl.BlockSpec((D_MODEL, FC), lambda b, j: (0, j)),
                  pl.BlockSpec((D_MODEL, FC), lambda b, j: (0, j + nchunk)),
                  pl.BlockSpec((FFN_CONV, FC), lambda b, j: (0, j)),
                  pl.BlockSpec((1, FC), lambda b, j: (0, j)),
                  pl.BlockSpec((FC, D_MODEL), lambda b, j: (j, 0)),
                  _const_spec((1, D_MODEL))],
        out_specs=seq,
        out_shape=jax.ShapeDtypeStruct((B, S, D_MODEL), F32),
        scratch_shapes=[pltpu.VMEM((S, D_MODEL), BF16)],
        compiler_params=_cparams(("parallel", "arbitrary"), 52),
        name="ffn",
    )(h, g, w_up, w_up, conv_w, conv_b, w_down, g_final)


def kernel(x, mem, norm_mix_g, w_in, mlstm_conv_w, mlstm_conv_b, mlstm_gate_b, mlstm_head_g,
           w_mix_out, norm_xattn_g, norm_mem_g, w_xq, w_xkv, w_xo, norm_ffn_g, w_ffn_up,
           ffn_conv_w, ffn_conv_b, w_ffn_down, norm_final_g):
    B, S, D = x.shape
    depth = w_in.shape[0]
    row1 = lambda v: v.reshape(1, -1)
    h = x
    for l in range(depth):
        w_in_p = jnp.pad(w_in[l], ((0, 0), (0, GATE_PAD - 2 * N_MLSTM_HEADS))).astype(BF16)
        aqkv, mqk, mvo, gates = _inproj(h.reshape(B * S, D), row1(norm_mix_g[l]), w_in_p)
        attn = _attn(aqkv.reshape(B, S, -1))
        hm = _mlstm(mqk.reshape(B, S, -1), mvo.reshape(B, S, -1), gates.reshape(B, S, -1),
                    mlstm_gate_b[l], mlstm_conv_w[l], mlstm_conv_b[l], mlstm_head_g[l])
        h = _outproj(h.reshape(B * S, D), attn.reshape(B * S, -1), hm.reshape(B * S, -1),
                     w_mix_out[l].astype(BF16)).reshape(B, S, D)
        k, v = _memkv(mem, row1(norm_mem_g[l]), w_xkv[l].astype(BF16))
        h = _xattn(h, row1(norm_xattn_g[l]), w_xq[l].astype(BF16), k, v, w_xo[l].astype(BF16))
        h = _ffn(h, row1(norm_ffn_g[l]), w_ffn_up[l].astype(BF16), ffn_conv_w[l],
                 row1(ffn_conv_b[l]), w_ffn_down[l].astype(BF16), row1(norm_final_g),
                 final_norm=l == depth - 1)
    return h
```

```python
import functools
import math

import jax
import jax.numpy as jnp
from jax import lax
from jax.experimental import pallas as pl
from jax.experimental.pallas import tpu as pltpu

F32 = jnp.float32
BF16 = jnp.bfloat16

D_MODEL = 1024
ATTN_WIDTH = 512
HEAD_DIM_ATTN = 64
N_ATTN_HEADS = 8
DILATIONS = (1, 4, 16)
BAND_BLOCK = 128
MLSTM_WIDTH = 512
N_MLSTM_HEADS = 4
HEAD_DIM_MLSTM = 128
MLSTM_CONV = 4
N_XATTN_HEADS = 4
HEAD_DIM_XATTN = 256
D_FF = 2816
FFN_CONV = 3
RMS_EPS = 1e-6
NEG_INF = -1e30
LOG2E = math.log2(math.e)

LANES = 128
SUBLANES = 8
GATE_PAD = LANES
MLSTM_CHUNK = 256
ROW_TILE = 512
FFN_CHUNK = 256
FFN_ROWS = 256
MIB = 1024 * 1024


def _cparams(semantics, vmem_mib):
    return pltpu.CompilerParams(dimension_semantics=semantics, vmem_limit_bytes=vmem_mib * MIB)


def _rms(x, g):
    return x * lax.rsqrt(jnp.mean(x * x, axis=-1, keepdims=True) + RMS_EPS) * g


def _dot(a, b):
    return jnp.dot(a, b, preferred_element_type=F32)


def _dot_nt(a, b):
    return lax.dot_general(a, b, (((1,), (1,)), ((), ())), preferred_element_type=F32)


def _sigmoid(x):
    return 1.0 / (1.0 + jnp.exp(-x))


def _log_sigmoid(x):
    return jnp.minimum(x, 0.0) - jnp.log(1.0 + jnp.exp(-jnp.abs(x)))


def _const_spec(shape):
    return pl.BlockSpec(shape, lambda *_: (0,) * len(shape))


def _causal_taps(x, tail, n_taps):
    R = x.shape[0]
    row = lax.broadcasted_iota(jnp.int32, x.shape, 0)
    taps = [x]
    for j in range(1, n_taps):
        head = jnp.tile(pltpu.roll(tail, j, 0), (R // SUBLANES, 1))
        taps.append(jnp.where(row < j, head, pltpu.roll(x, j, 0)))
    return taps


def _lane_scan(x, op, fill):
    lane = lax.broadcasted_iota(jnp.int32, x.shape, 1)
    k = 1
    while k < x.shape[1]:
        x = op(x, jnp.where(lane >= k, pltpu.roll(x, k, 1), fill))
        k *= 2
    return x


def _inproj_kernel(x_ref, g_ref, w_ref, cw_ref, cb_ref, aqkv_ref, mqk_ref, mvo_ref, gates_ref, tail_ref):
    tm = x_ref.shape[1]
    A, M = ATTN_WIDTH, MLSTM_WIDTH

    @pl.when(pl.program_id(1) == 0)
    def _():
        tail_ref[...] = jnp.zeros_like(tail_ref)

    xn = _rms(x_ref[0], g_ref[...]).astype(BF16)

    def mm(c0, c1):
        return _dot(xn, w_ref[:, c0:c1])

    o = 3 * A
    raws = [mm(o, o + M), mm(o + M, o + 2 * M)]
    for half, raw in enumerate(raws):
        cols = slice(half * M, (half + 1) * M)
        acc = cb_ref[:, cols]
        for j, t in enumerate(_causal_taps(raw, tail_ref[:, cols], MLSTM_CONV)):
            acc = acc + cw_ref[j:j + 1, cols] * t
        tail_ref[:, cols] = raw[tm - SUBLANES:tm, :]
        y = acc * _sigmoid(acc)
        if half == 1:
            y = y * (HEAD_DIM_MLSTM ** -0.5)
        mqk_ref[0, :, cols] = y.astype(BF16)
    aqkv_ref[0, :, 0:A] = mm(0, A) * (HEAD_DIM_ATTN ** -0.5)
    aqkv_ref[0, :, A:2 * A] = mm(A, 2 * A)
    aqkv_ref[0, :, 2 * A:3 * A] = mm(2 * A, 3 * A)
    mvo_ref[0, :, 0:M] = mm(o + 2 * M, o + 3 * M).astype(BF16)
    mvo_ref[0, :, M:2 * M] = _sigmoid(mm(o + 3 * M, o + 4 * M)).astype(BF16)
    gates_ref[0] = mm(o + 4 * M, o + 4 * M + GATE_PAD)


def _inproj(x, g, w, conv_w, conv_b):
    B, S, _ = x.shape
    tm = ROW_TILE
    ncol = w.shape[1]
    M2 = 2 * MLSTM_WIDTH
    rows = lambda wd: pl.BlockSpec((1, tm, wd), lambda b, i: (b, i, 0))
    return pl.pallas_call(
        _inproj_kernel,
        grid=(B, S // tm),
        in_specs=[rows(D_MODEL), _const_spec((1, D_MODEL)), _const_spec((D_MODEL, ncol)),
                  _const_spec((MLSTM_CONV, M2)), _const_spec((1, M2))],
        out_specs=[rows(3 * ATTN_WIDTH), rows(M2), rows(M2), rows(GATE_PAD)],
        out_shape=[jax.ShapeDtypeStruct((B, S, 3 * ATTN_WIDTH), F32),
                   jax.ShapeDtypeStruct((B, S, M2), BF16),
                   jax.ShapeDtypeStruct((B, S, M2), BF16),
                   jax.ShapeDtypeStruct((B, S, GATE_PAD), F32)],
        scratch_shapes=[pltpu.VMEM((SUBLANES, M2), F32)],
        compiler_params=_cparams(("parallel", "arbitrary"), 48),
        name="inproj",
    )(x, g, w, conv_w, conv_b)


def _alibi_tables():
    qi = jnp.arange(BAND_BLOCK)[:, None]
    kj = jnp.arange(2 * BAND_BLOCK)[None, :]
    rel = BAND_BLOCK + qi - kj
    valid = (rel >= 0) & (rel <= BAND_BLOCK)
    h = jnp.arange(1, N_ATTN_HEADS + 1, dtype=F32)
    slopes = jnp.exp2(-8.0 * h / N_ATTN_HEADS)
    tabs = []
    for d in DILATIONS:
        dist = (rel * d).astype(F32)
        tabs.append(jnp.where(valid[None], -slopes[:, None, None] * dist[None] * LOG2E, NEG_INF))
    return jnp.stack(tabs)


def _attn_kernel(q_ref, k_ref, v_ref, tab_ref, out_ref,
                 qlo_s, qhi_s, k_s, v_s, acc_s, m_s, l_s):
    S = q_ref.shape[1]
    B = BAND_BLOCK
    lane1 = lax.broadcasted_iota(jnp.int32, (1, LANES), 1)
    mlo = jnp.where(lane1 < HEAD_DIM_ATTN, LOG2E, 0.0).astype(F32)
    mhi = LOG2E - mlo
    lo = lax.broadcasted_iota(jnp.int32, (B, LANES), 1) < HEAD_DIM_ATTN

    for bi, d in enumerate(DILATIONS):
        U = S // d
        for r in range(d):
            src = pl.ds(r, U, stride=d) if d > 1 else pl.ds(0, S)
            dst = pl.ds(r * U, U)
            qf = q_ref[0, src, :]
            qlo_s[bi, dst, :] = (qf * mlo).astype(BF16)
            qhi_s[bi, dst, :] = (qf * mhi).astype(BF16)
            k_s[bi, dst, :] = k_ref[0, src, :].astype(BF16)
            v_s[bi, dst, :] = v_ref[0, src, :].astype(BF16)

    def group(bi, row0s, starts, has_prev):
        d = DILATIONS[bi]
        scores = []
        for row0 in row0s:
            qrow = pl.ds(row0, B)
            krow = pl.ds(row0 - B, 2 * B) if has_prev else qrow
            k2 = k_s[bi, krow, :]
            v2 = v_s[bi, krow, :]
            scores.append((_dot_nt(qlo_s[bi, qrow, :], k2), v2, 0))
            scores.append((_dot_nt(qhi_s[bi, qrow, :], k2), v2, 1))
        soft = []
        for s, v2, e in scores:
            t = tab_ref[bi, e] if has_prev else tab_ref[bi, e, :, B:2 * B]
            s = s + t
            m = jnp.max(s, axis=-1, keepdims=True)
            pe = jnp.exp2(s - m)
            soft.append((m, jnp.sum(pe, axis=-1, keepdims=True), pe.astype(BF16), v2))
        pvs = [_dot(pe, v2) for _, _, pe, v2 in soft]
        for i, start in enumerate(starts):
            m = jnp.where(lo, soft[2 * i][0], soft[2 * i + 1][0])
            l = jnp.where(lo, soft[2 * i][1], soft[2 * i + 1][1])
            pv = jnp.where(lo, pvs[2 * i], pvs[2 * i + 1])
            pos = pl.ds(start, B, stride=d) if d > 1 else pl.ds(start, B)
            if bi == 0:
                m_s[pos, :] = m
                l_s[pos, :] = l
                acc_s[pos, :] = pv
            else:
                m_old = m_s[pos, :]
                m_new = jnp.maximum(m_old, m)
                a_old = jnp.exp2(m_old - m_new)
                a_blk = jnp.exp2(m - m_new)
                m_s[pos, :] = m_new
                l_s[pos, :] = a_old * l_s[pos, :] + a_blk * l
                acc_s[pos, :] = a_old * acc_s[pos, :] + a_blk * pv

    al = lambda v: pl.multiple_of(v, B)

    group(0, [0], [0], False)

    def d1_body(g, c):
        offs = [al((1 + 5 * g + i) * B) for i in range(5)]
        group(0, offs, offs, True)
        return c
    lax.fori_loop(0, 3, d1_body, 0)

    U4 = S // 4
    group(1, [r * U4 for r in range(4)], list(range(4)), False)

    def d4_body(n, c):
        group(1, [al(r * U4 + n * B) for r in range(4)], [r + n * (4 * B) for r in range(4)], True)
        return c
    lax.fori_loop(1, U4 // B, d4_body, 0)

    def d16_body(g, c):
        group(2, [al((4 * g + i) * B) for i in range(4)], [4 * g + i for i in range(4)], False)
        return c
    lax.fori_loop(0, 4, d16_body, 0)

    out_ref[0] = (acc_s[...] / l_s[...]).astype(BF16)


def _attn(aqkv):
    B, S, _ = aqkv.shape
    npair = N_ATTN_HEADS // 2
    assert S == 16 * BAND_BLOCK, "block schedule below is written for 16 blocks of 128"
    tabs = _alibi_tables()
    seq = lambda off: pl.BlockSpec((1, S, LANES), lambda b, p: (b, 0, p + off))
    return pl.pallas_call(
        _attn_kernel,
        grid=(B, npair),
        in_specs=[seq(0), seq(npair), seq(2 * npair),
                  pl.BlockSpec((len(DILATIONS), 2, BAND_BLOCK, 2 * BAND_BLOCK), lambda b, p: (0, p, 0, 0))],
        out_specs=seq(0),
        out_shape=jax.ShapeDtypeStruct((B, S, ATTN_WIDTH), BF16),
        scratch_shapes=[pltpu.VMEM((len(DILATIONS), S, LANES), BF16)] * 4
                       + [pltpu.VMEM((S, LANES), F32)] * 3,
        compiler_params=_cparams(("parallel", "parallel"), 32),
        name="attn",
    )(aqkv, aqkv, aqkv, tabs)


def _mlstm_kernel(qk_ref, vo_ref, gr_ref, gb_ref, hg_ref, out_ref, rv_s, cm_s, b_s, cols1_s, cols2_s):
    S = qk_ref.shape[1]
    L = MLSTM_CHUNK
    nc = S // L
    H, dh, W = N_MLSTM_HEADS, HEAD_DIM_MLSTM, MLSTM_WIDTH
    G = 2 * H
    ti = lax.broadcasted_iota(jnp.int32, (L, L), 0)
    si = lax.broadcasted_iota(jnp.int32, (L, L), 1)
    causal = si <= ti

    X = gr_ref[...] + gb_ref[...]
    is_li = (lax.broadcasted_iota(jnp.int32, X.shape, 0) & (G - 1)) < H
    Bc = _lane_scan(_log_sigmoid(X), jnp.add, 0.0)
    Bup = pltpu.roll(Bc, nc * G - H, 0)
    RV = X - Bup
    CM = _lane_scan(RV, jnp.maximum, NEG_INF)
    A1 = jnp.where(is_li, RV, pltpu.roll(CM, H, 0))
    rv_s[...] = RV
    cm_s[...] = CM
    b_s[...] = Bup
    for c in range(nc):
        cols1_s[c] = A1[c * G:(c + 1) * G, :].T
        cols2_s[c] = Bup[c * G:(c + 1) * G, :].T

    def chunk(c, carry):
        rows = pl.ds(pl.multiple_of(c * L, L), L)
        grow = pl.ds(pl.multiple_of(c * G, G), G)
        RV = rv_s[grow, :]
        CM = cm_s[grow, :]
        Bup = b_s[grow, :]
        cols1 = cols1_s[c]
        cols2 = cols2_s[c]
        prep, new_carry = [], []
        ones = jnp.ones((L, dh), BF16)
        for h in range(H):
            Cn, m = carry[h]
            col = slice(h * dh, (h + 1) * dh)
            qc = qk_ref[0, rows, col]
            kc = qk_ref[0, rows, W + h * dh:W + (h + 1) * dh]
            v1 = jnp.concatenate([vo_ref[0, rows, col], ones], axis=1)
            qk = _dot_nt(qc, kc)
            qCn = _dot(qc, Cn.astype(BF16))
            g = Bup[h:h + 1, L - 1:L]
            cm_last = CM[h:h + 1, L - 1:L]
            kwT = kc.astype(F32).T * jnp.exp(RV[h:h + 1, :] - cm_last)
            kvn = _dot(kwT.astype(BF16), v1)
            prep.append((v1, qk, qCn, m))
            m_loc = g + cm_last
            m_new = jnp.maximum(g + m, m_loc)
            new_carry.append((jnp.exp(g + m - m_new) * Cn + jnp.exp(m_loc - m_new) * kvn, m_new))
        mid = []
        for h, (v1, qk, qCn, m) in enumerate(prep):
            mm = jnp.maximum(jnp.broadcast_to(cols1[:, H + h:H + h + 1], (L, dh)), m)
            b_t = jnp.broadcast_to(cols2[:, h:h + 1], (L, dh))
            mm2 = jnp.concatenate([mm] * (L // dh), axis=1)
            Sm = jnp.exp(jnp.where(causal, RV[h:h + 1, :] - mm2, NEG_INF)) * qk
            intra = _dot(Sm.astype(BF16), v1)
            inter_w = jnp.exp(m - mm)
            den = inter_w * qCn[:, dh:] + intra[:, dh:]
            mid.append((inter_w * qCn[:, :dh] + intra[:, :dh],
                        jnp.maximum(jnp.abs(den), jnp.exp(-(b_t + mm)))))
        for h, (num, dn) in enumerate(mid):
            col = slice(h * dh, (h + 1) * dh)
            hh = num / dn
            hn = hh * lax.rsqrt(jnp.mean(hh * hh, axis=-1, keepdims=True) + RMS_EPS) * hg_ref[:, col]
            og = vo_ref[0, rows, W + h * dh:W + (h + 1) * dh].astype(F32)
            out_ref[0, rows, col] = (hn * og).astype(BF16)
        return tuple(new_carry)

    init = tuple((jnp.zeros((dh, 2 * dh), F32), jnp.zeros((1, 1), F32)) for _ in range(H))
    lax.fori_loop(0, nc, chunk, init, unroll=2)


def _mlstm(mqk, mvo, gates, gate_b, head_g):
    B, S, _ = mqk.shape
    H, L, W = N_MLSTM_HEADS, MLSTM_CHUNK, MLSTM_WIDTH
    nc = S // L
    G = 2 * H
    gates_row = gates[:, :, :G].reshape(B, nc, L, G).transpose(0, 1, 3, 2).reshape(B, nc * G, L)
    gb = jnp.tile(gate_b.reshape(G, 1), (nc, 1))
    seq = lambda w: pl.BlockSpec((1, S, w), lambda b: (b, 0, 0))
    return pl.pallas_call(
        _mlstm_kernel,
        grid=(B,),
        in_specs=[seq(2 * W), seq(2 * W),
                  pl.BlockSpec((None, nc * G, L), lambda b: (b, 0, 0)),
                  _const_spec((nc * G, 1)), _const_spec((1, W))],
        out_specs=seq(W),
        out_shape=jax.ShapeDtypeStruct((B, S, W), BF16),
        scratch_shapes=[pltpu.VMEM((nc * G, L), F32)] * 3 + [pltpu.VMEM((nc, L, G), F32)] * 2,
        compiler_params=_cparams(("parallel",), 40),
        name="mlstm",
    )(mqk, mvo, gates_row, gb, head_g.reshape(1, W))


def _memkv_kernel(mem_ref, g_ref, w_ref, k_ref, v_ref):
    mn = _rms(mem_ref[0], g_ref[...]).astype(BF16)
    k_ref[0] = _dot(mn, w_ref[:, 0:D_MODEL]).astype(BF16)
    v_ref[0] = _dot(mn, w_ref[:, D_MODEL:]).astype(BF16)


def _memkv(mem, g, w):
    B, M, _ = mem.shape
    blk = pl.BlockSpec((1, M, D_MODEL), lambda b: (b, 0, 0))
    return pl.pallas_call(
        _memkv_kernel,
        grid=(B,),
        in_specs=[blk, _const_spec((1, D_MODEL)), _const_spec((D_MODEL, 2 * D_MODEL))],
        out_specs=[blk, blk],
        out_shape=[jax.ShapeDtypeStruct((B, M, D_MODEL), BF16)] * 2,
        compiler_params=_cparams(("parallel",), 32),
        name="memkv",
    )(mem, g, w)


def _mid_kernel(x_ref, attn_ref, hm_ref, wmix_ref, g_ref, wq_ref, k_ref, v_ref, wo_ref, out_ref, o_scr):
    A = ATTN_WIDTH
    h1 = x_ref[0] + _dot(attn_ref[0], wmix_ref[0:A, :]) + _dot(hm_ref[0], wmix_ref[A:, :])
    hn = _rms(h1, g_ref[...]).astype(BF16)
    q = (_dot(hn, wq_ref[...]) * (HEAD_DIM_XATTN ** -0.5)).astype(BF16)
    dh = HEAD_DIM_XATTN
    cols = [slice(hd * dh, (hd + 1) * dh) for hd in range(N_XATTN_HEADS)]
    scores = [_dot_nt(q[:, c], k_ref[0, :, c]) for c in cols]
    soft = []
    for s in scores:
        m = jnp.max(s, axis=-1, keepdims=True)
        pe = jnp.exp(s - m)
        soft.append((pe.astype(BF16), jnp.sum(pe, axis=-1, keepdims=True)))
    pvs = [_dot(pe, v_ref[0, :, c]) for (pe, _), c in zip(soft, cols)]
    for pv, (_, l), c in zip(pvs, soft, cols):
        o_scr[:, c] = (pv / l).astype(BF16)
    out_ref[0] = h1 + _dot(o_scr[...], wo_ref[...])


def _mid(x, attn, hm, wmix, g, wq, k, v, wo):
    B, S, _ = x.shape
    M = k.shape[1]
    tm = ROW_TILE
    rows = lambda w: pl.BlockSpec((1, tm, w), lambda b, i: (b, i, 0))
    kvb = pl.BlockSpec((1, M, D_MODEL), lambda b, i: (b, 0, 0))
    sq = _const_spec((D_MODEL, D_MODEL))
    return pl.pallas_call(
        _mid_kernel,
        grid=(B, S // tm),
        in_specs=[rows(D_MODEL), rows(ATTN_WIDTH), rows(MLSTM_WIDTH), sq,
                  _const_spec((1, D_MODEL)), sq, kvb, kvb, sq],
        out_specs=rows(D_MODEL),
        out_shape=jax.ShapeDtypeStruct((B, S, D_MODEL), F32),
        scratch_shapes=[pltpu.VMEM((tm, D_MODEL), BF16)],
        compiler_params=_cparams(("parallel", "parallel"), 48),
        name="mid",
    )(x, attn, hm, wmix, g, wq, k, v, wo)


def _ffn_kernel(h_ref, g_ref, wg_ref, wu_ref, cw_ref, cb_ref, wd_ref, gf_ref, out_ref, hn_ref,
                *, final_norm):
    j = pl.program_id(1)
    S = h_ref.shape[1]
    R = FFN_ROWS
    nblk = S // R

    @pl.when(j == 0)
    def _():
        hres = h_ref[0]
        hn_ref[...] = _rms(hres, g_ref[...]).astype(BF16)
        out_ref[0] = hres

    def up_dots(i):
        hn = hn_ref[i * R:(i + 1) * R, :]
        return _dot(hn, wg_ref[...]), _dot(hn, wu_ref[...])

    nxt = up_dots(0)
    tail = jnp.zeros((SUBLANES, FFN_CHUNK), F32)
    for i in range(nblk):
        gate, up = nxt
        if i + 1 < nblk:
            nxt = up_dots(i + 1)
        cv = cb_ref[...]
        for jt, t in enumerate(_causal_taps(gate, tail, FFN_CONV)):
            cv = cv + cw_ref[jt:jt + 1, :] * t
        tail = gate[R - SUBLANES:R, :]
        act = (cv * _sigmoid(cv) * up).astype(BF16)
        out_ref[0, i * R:(i + 1) * R, :] += _dot(act, wd_ref[...])

    if final_norm:
        @pl.when(j == pl.num_programs(1) - 1)
        def _():
            out_ref[0] = _rms(out_ref[0], gf_ref[...])


def _ffn(h, g, w_up, conv_w, conv_b, w_down, g_final, final_norm):
    B, S, _ = h.shape
    FC = FFN_CHUNK
    nchunk = D_FF // FC
    w_up_c = w_up.reshape(D_MODEL, 2 * nchunk, FC).transpose(1, 0, 2)
    seq = pl.BlockSpec((1, S, D_MODEL), lambda b, j: (b, 0, 0))
    return pl.pallas_call(
        functools.partial(_ffn_kernel, final_norm=final_norm),
        grid=(B, nchunk),
        in_specs=[seq, _const_spec((1, D_MODEL)),
                  pl.BlockSpec((None, D_MODEL, FC), lambda b, j: (j, 0, 0)),
                  pl.BlockSpec((None, D_MODEL, FC), lambda b, j: (j + nchunk, 0, 0)),
                  pl.BlockSpec((FFN_CONV, FC), lambda b, j: (0, j)),
                  pl.BlockSpec((1, FC), lambda b, j: (0, j)),
                  pl.BlockSpec((FC, D_MODEL), lambda b, j: (j, 0)),
                  _const_spec((1, D_MODEL))],
        out_specs=seq,
        out_shape=jax.ShapeDtypeStruct((B, S, D_MODEL), F32),
        scratch_shapes=[pltpu.VMEM((S, D_MODEL), BF16)],
        compiler_params=_cparams(("parallel", "arbitrary"), 52),
        name="ffn",
    )(h, g, w_up_c, w_up_c, conv_w, conv_b, w_down, g_final)


def kernel(x, mem, norm_mix_g, w_in, mlstm_conv_w, mlstm_conv_b, mlstm_gate_b, mlstm_head_g,
           w_mix_out, norm_xattn_g, norm_mem_g, w_xq, w_xkv, w_xo, norm_ffn_g, w_ffn_up,
           ffn_conv_w, ffn_conv_b, w_ffn_down, norm_final_g):
    depth = w_in.shape[0]
    row1 = lambda v: v.reshape(1, -1)
    h = x
    for l in range(depth):
        w_in_p = jnp.pad(w_in[l], ((0, 0), (0, GATE_PAD - 2 * N_MLSTM_HEADS))).astype(BF16)
        aqkv, mqk, mvo, gates = _inproj(h, row1(norm_mix_g[l]), w_in_p,
                                        mlstm_conv_w[l], row1(mlstm_conv_b[l]))
        attn = _attn(aqkv)
        hm = _mlstm(mqk, mvo, gates, mlstm_gate_b[l], mlstm_head_g[l])
        k, v = _memkv(mem, row1(norm_mem_g[l]), w_xkv[l].astype(BF16))
        h = _mid(h, attn, hm, w_mix_out[l].astype(BF16), row1(norm_xattn_g[l]),
                 w_xq[l].astype(BF16), k, v, w_xo[l].astype(BF16))
        h = _ffn(h, row1(norm_ffn_g[l]), w_ffn_up[l].astype(BF16), ffn_conv_w[l],
                 row1(ffn_conv_b[l]), w_ffn_down[l].astype(BF16), row1(norm_final_g),
                 final_norm=l == depth - 1)
    return h
```

```python
import functools
import math

import jax
import jax.numpy as jnp
from jax import lax
from jax.experimental import pallas as pl
from jax.experimental.pallas import tpu as pltpu

F32 = jnp.float32
BF16 = jnp.bfloat16

D_MODEL = 1024
ATTN_WIDTH = 512
HEAD_DIM_ATTN = 64
N_ATTN_HEADS = 8
DILATIONS = (1, 4, 16)
BAND_BLOCK = 128
MLSTM_WIDTH = 512
N_MLSTM_HEADS = 4
HEAD_DIM_MLSTM = 128
MLSTM_CONV = 4
N_XATTN_HEADS = 4
HEAD_DIM_XATTN = 256
D_FF = 2816
FFN_CONV = 3
RMS_EPS = 1e-6
NEG_INF = -1e30
LOG2E = math.log2(math.e)

LANES = 128
SUBLANES = 8
GATE_PAD = LANES
MLSTM_CHUNK = 256
ROW_TILE = 512
FFN_CHUNK = 256
FFN_ROWS = 256
MIB = 1024 * 1024


def _cparams(semantics, vmem_mib):
    return pltpu.CompilerParams(dimension_semantics=semantics, vmem_limit_bytes=vmem_mib * MIB)


def _rms(x, g):
    return x * lax.rsqrt(jnp.mean(x * x, axis=-1, keepdims=True) + RMS_EPS) * g


def _dot(a, b):
    return jnp.dot(a, b, preferred_element_type=F32)


def _dot_nt(a, b):
    return lax.dot_general(a, b, (((1,), (1,)), ((), ())), preferred_element_type=F32)


def _sigmoid(x):
    return 1.0 / (1.0 + jnp.exp(-x))


def _log_sigmoid(x):
    return jnp.minimum(x, 0.0) - jnp.log(1.0 + jnp.exp(-jnp.abs(x)))


def _const_spec(shape):
    return pl.BlockSpec(shape, lambda *_: (0,) * len(shape))


def _causal_taps(x, tail, n_taps):
    R = x.shape[0]
    row = lax.broadcasted_iota(jnp.int32, x.shape, 0)
    taps = [x]
    for j in range(1, n_taps):
        head = jnp.tile(pltpu.roll(tail, j, 0), (R // SUBLANES, 1))
        taps.append(jnp.where(row < j, head, pltpu.roll(x, j, 0)))
    return taps


def _lane_scan(x, op, fill):
    lane = lax.broadcasted_iota(jnp.int32, x.shape, 1)
    k = 1
    while k < x.shape[1]:
        x = op(x, jnp.where(lane >= k, pltpu.roll(x, k, 1), fill))
        k *= 2
    return x


def _inproj_kernel(x_ref, g_ref, w_ref, cw_ref, cb_ref, aqkv_ref, mqk_ref, mvo_ref, gates_ref, tail_ref):
    tm = x_ref.shape[1]
    A, M = ATTN_WIDTH, MLSTM_WIDTH

    @pl.when(pl.program_id(1) == 0)
    def _():
        tail_ref[...] = jnp.zeros_like(tail_ref)

    xn = _rms(x_ref[0], g_ref[...]).astype(BF16)

    def mm(c0, c1):
        return _dot(xn, w_ref[:, c0:c1])

    o = 3 * A
    raws = [mm(o, o + M), mm(o + M, o + 2 * M)]
    for half, raw in enumerate(raws):
        cols = slice(half * M, (half + 1) * M)
        acc = cb_ref[:, cols]
        for j, t in enumerate(_causal_taps(raw, tail_ref[:, cols], MLSTM_CONV)):
            acc = acc + cw_ref[j:j + 1, cols] * t
        tail_ref[:, cols] = raw[tm - SUBLANES:tm, :]
        y = acc * _sigmoid(acc)
        if half == 1:
            y = y * (HEAD_DIM_MLSTM ** -0.5)
        mqk_ref[0, :, cols] = y.astype(BF16)
    aqkv_ref[0, :, 0:A] = mm(0, A) * (HEAD_DIM_ATTN ** -0.5 * LOG2E)
    aqkv_ref[0, :, A:2 * A] = mm(A, 2 * A)
    aqkv_ref[0, :, 2 * A:3 * A] = mm(2 * A, 3 * A)
    mvo_ref[0, :, 0:M] = mm(o + 2 * M, o + 3 * M).astype(BF16)
    mvo_ref[0, :, M:2 * M] = _sigmoid(mm(o + 3 * M, o + 4 * M)).astype(BF16)
    gates_ref[0] = mm(o + 4 * M, o + 4 * M + GATE_PAD)


def _inproj(x, g, w, conv_w, conv_b):
    B, S, _ = x.shape
    tm = ROW_TILE
    ncol = w.shape[1]
    M2 = 2 * MLSTM_WIDTH
    rows = lambda wd: pl.BlockSpec((1, tm, wd), lambda b, i: (b, i, 0))
    return pl.pallas_call(
        _inproj_kernel,
        grid=(B, S // tm),
        in_specs=[rows(D_MODEL), _const_spec((1, D_MODEL)), _const_spec((D_MODEL, ncol)),
                  _const_spec((MLSTM_CONV, M2)), _const_spec((1, M2))],
        out_specs=[rows(3 * ATTN_WIDTH), rows(M2), rows(M2), rows(GATE_PAD)],
        out_shape=[jax.ShapeDtypeStruct((B, S, 3 * ATTN_WIDTH), F32),
                   jax.ShapeDtypeStruct((B, S, M2), BF16),
                   jax.ShapeDtypeStruct((B, S, M2), BF16),
                   jax.ShapeDtypeStruct((B, S, GATE_PAD), F32)],
        scratch_shapes=[pltpu.VMEM((SUBLANES, M2), F32)],
        compiler_params=_cparams(("parallel", "arbitrary"), 48),
        name="inproj",
    )(x, g, w, conv_w, conv_b)


def _alibi_tables():
    qi = jnp.arange(BAND_BLOCK)[:, None]
    kj = jnp.arange(2 * BAND_BLOCK)[None, :]
    rel = BAND_BLOCK + qi - kj
    valid = (rel >= 0) & (rel <= BAND_BLOCK)
    h = jnp.arange(1, N_ATTN_HEADS + 1, dtype=F32)
    slopes = jnp.exp2(-8.0 * h / N_ATTN_HEADS)
    tabs = []
    for d in DILATIONS:
        dist = (rel * d).astype(F32)
        tabs.append(jnp.where(valid[None], -slopes[:, None, None] * dist[None] * LOG2E, NEG_INF))
    return jnp.stack(tabs)


def _attn_kernel(q_ref, k_ref, v_ref, tab_ref, out_ref, q_s, k_s, v_s, acc_s, m_s, l_s):
    S = q_ref.shape[1]
    B = BAND_BLOCK
    lo = lax.broadcasted_iota(jnp.int32, (B, LANES), 1) < HEAD_DIM_ATTN

    for bi, d in enumerate(DILATIONS):
        U = S // d
        for r in range(d):
            src = pl.ds(r, U, stride=d) if d > 1 else pl.ds(0, S)
            dst = pl.ds(r * U, U)
            q_s[bi, dst, :] = q_ref[0, src, :].astype(BF16)
            k_s[bi, dst, :] = k_ref[0, src, :].astype(BF16)
            v_s[bi, dst, :] = v_ref[0, src, :].astype(BF16)

    def group(bi, row0s, starts, has_prev):
        d = DILATIONS[bi]
        scores = []
        for row0 in row0s:
            qrow = pl.ds(row0, B)
            krow = pl.ds(row0 - B, 2 * B) if has_prev else qrow
            q2 = q_s[bi, qrow, :]
            k2 = k_s[bi, krow, :]
            v2 = v_s[bi, krow, :]
            zero = jnp.zeros_like(q2)
            scores.append((_dot_nt(jnp.where(lo, q2, zero), k2), v2, 0))
            scores.append((_dot_nt(jnp.where(lo, zero, q2), k2), v2, 1))
        soft = []
        for s, v2, e in scores:
            t = tab_ref[bi, e] if has_prev else tab_ref[bi, e, :, B:2 * B]
            s = s + t
            m = jnp.max(s, axis=-1, keepdims=True)
            pe = jnp.exp2(s - m)
            soft.append((m, jnp.sum(pe, axis=-1, keepdims=True), pe.astype(BF16), v2))
        pvs = [_dot(pe, v2) for _, _, pe, v2 in soft]
        for i, start in enumerate(starts):
            m = jnp.where(lo, soft[2 * i][0], soft[2 * i + 1][0])
            l = jnp.where(lo, soft[2 * i][1], soft[2 * i + 1][1])
            pv = jnp.where(lo, pvs[2 * i], pvs[2 * i + 1])
            pos = pl.ds(start, B, stride=d) if d > 1 else pl.ds(start, B)
            if bi == 0:
                m_s[pos, :] = m
                l_s[pos, :] = l
                acc_s[pos, :] = pv
            else:
                m_old = m_s[pos, :]
                m_new = jnp.maximum(m_old, m)
                a_old = jnp.exp2(m_old - m_new)
                a_blk = jnp.exp2(m - m_new)
                m_s[pos, :] = m_new
                l_s[pos, :] = a_old * l_s[pos, :] + a_blk * l
                acc_s[pos, :] = a_old * acc_s[pos, :] + a_blk * pv

    al = lambda v: pl.multiple_of(v, B)

    group(0, [0], [0], False)

    def d1_body(g, c):
        offs = [al((1 + 5 * g + i) * B) for i in range(5)]
        group(0, offs, offs, True)
        return c
    lax.fori_loop(0, 3, d1_body, 0)

    U4 = S // 4
    group(1, [r * U4 for r in range(4)], list(range(4)), False)

    def d4_body(n, c):
        group(1, [al(r * U4 + n * B) for r in range(4)], [r + n * (4 * B) for r in range(4)], True)
        return c
    lax.fori_loop(1, U4 // B, d4_body, 0)

    def d16_body(g, c):
        group(2, [al((4 * g + i) * B) for i in range(4)], [4 * g + i for i in range(4)], False)
        return c
    lax.fori_loop(0, 4, d16_body, 0)

    out_ref[0] = (acc_s[...] / l_s[...]).astype(BF16)


def _attn(aqkv):
    B, S, _ = aqkv.shape
    npair = N_ATTN_HEADS // 2
    assert S == 16 * BAND_BLOCK, "block schedule below is written for 16 blocks of 128"
    tabs = _alibi_tables()
    seq = lambda off: pl.BlockSpec((1, S, LANES), lambda b, p: (b, 0, p + off))
    return pl.pallas_call(
        _attn_kernel,
        grid=(B, npair),
        in_specs=[seq(0), seq(npair), seq(2 * npair),
                  pl.BlockSpec((len(DILATIONS), 2, BAND_BLOCK, 2 * BAND_BLOCK), lambda b, p: (0, p, 0, 0))],
        out_specs=seq(0),
        out_shape=jax.ShapeDtypeStruct((B, S, ATTN_WIDTH), BF16),
        scratch_shapes=[pltpu.VMEM((len(DILATIONS), S, LANES), BF16)] * 3
                       + [pltpu.VMEM((S, LANES), F32)] * 3,
        compiler_params=_cparams(("parallel", "parallel"), 32),
        name="attn",
    )(aqkv, aqkv, aqkv, tabs)


def _mlstm_kernel(qk_ref, vo_ref, gr_ref, gb_ref, hg_ref, out_ref, rv_s, cm_s, b_s, cols1_s, cols2_s):
    S = qk_ref.shape[1]
    L = MLSTM_CHUNK
    nc = S // L
    H, dh, W = N_MLSTM_HEADS, HEAD_DIM_MLSTM, MLSTM_WIDTH
    G = 2 * H
    ti = lax.broadcasted_iota(jnp.int32, (L, L), 0)
    si = lax.broadcasted_iota(jnp.int32, (L, L), 1)
    causal = si <= ti

    X = gr_ref[...] + gb_ref[...]
    is_li = (lax.broadcasted_iota(jnp.int32, X.shape, 0) & (G - 1)) < H
    Bc = _lane_scan(_log_sigmoid(X), jnp.add, 0.0)
    Bup = pltpu.roll(Bc, nc * G - H, 0)
    RV = X - Bup
    CM = _lane_scan(RV, jnp.maximum, NEG_INF)
    A1 = jnp.where(is_li, RV, pltpu.roll(CM, H, 0))
    rv_s[...] = RV
    cm_s[...] = CM
    b_s[...] = Bup
    for c in range(nc):
        cols1_s[c] = A1[c * G:(c + 1) * G, :].T
        cols2_s[c] = Bup[c * G:(c + 1) * G, :].T

    def chunk(c, carry):
        rows = pl.ds(pl.multiple_of(c * L, L), L)
        grow = pl.ds(pl.multiple_of(c * G, G), G)
        RV = rv_s[grow, :]
        CM = cm_s[grow, :]
        Bup = b_s[grow, :]
        cols1 = cols1_s[c]
        cols2 = cols2_s[c]
        prep, new_carry = [], []
        ones = jnp.ones((L, dh), BF16)
        for h in range(H):
            Cn, m = carry[h]
            col = slice(h * dh, (h + 1) * dh)
            qc = qk_ref[0, rows, col]
            kc = qk_ref[0, rows, W + h * dh:W + (h + 1) * dh]
            v1 = jnp.concatenate([vo_ref[0, rows, col], ones], axis=1)
            qk = _dot_nt(qc, kc)
            qCn = _dot(qc, Cn.astype(BF16))
            g = Bup[h:h + 1, L - 1:L]
            cm_last = CM[h:h + 1, L - 1:L]
            kwT = kc.astype(F32).T * jnp.exp(RV[h:h + 1, :] - cm_last)
            kvn = _dot(kwT.astype(BF16), v1)
            prep.append((v1, qk, qCn, m))
            m_loc = g + cm_last
            m_new = jnp.maximum(g + m, m_loc)
            new_carry.append((jnp.exp(g + m - m_new) * Cn + jnp.exp(m_loc - m_new) * kvn, m_new))
        mid = []
        for h, (v1, qk, qCn, m) in enumerate(prep):
            mm = jnp.maximum(jnp.broadcast_to(cols1[:, H + h:H + h + 1], (L, dh)), m)
            b_t = jnp.broadcast_to(cols2[:, h:h + 1], (L, dh))
            mm2 = jnp.concatenate([mm] * (L // dh), axis=1)
            Sm = jnp.exp(jnp.where(causal, RV[h:h + 1, :] - mm2, NEG_INF)) * qk
            intra = _dot(Sm.astype(BF16), v1)
            inter_w = jnp.exp(m - mm)
            den = inter_w * qCn[:, dh:] + intra[:, dh:]
            mid.append((inter_w * qCn[:, :dh] + intra[:, :dh],
                        jnp.maximum(jnp.abs(den), jnp.exp(-(b_t + mm)))))
        for h, (num, dn) in enumerate(mid):
            col = slice(h * dh, (h + 1) * dh)
            hh = num / dn
            hn = hh * lax.rsqrt(jnp.mean(hh * hh, axis=-1, keepdims=True) + RMS_EPS) * hg_ref[:, col]
            og = vo_ref[0, rows, W + h * dh:W + (h + 1) * dh].astype(F32)
            out_ref[0, rows, col] = (hn * og).astype(BF16)
        return tuple(new_carry)

    init = tuple((jnp.zeros((dh, 2 * dh), F32), jnp.zeros((1, 1), F32)) for _ in range(H))
    lax.fori_loop(0, nc, chunk, init, unroll=2)


def _mlstm(mqk, mvo, gates, gate_b, head_g):
    B, S, _ = mqk.shape
    H, L, W = N_MLSTM_HEADS, MLSTM_CHUNK, MLSTM_WIDTH
    nc = S // L
    G = 2 * H
    gates_row = gates[:, :, :G].reshape(B, nc, L, G).transpose(0, 1, 3, 2).reshape(B, nc * G, L)
    gb = jnp.tile(gate_b.reshape(G, 1), (nc, 1))
    seq = lambda w: pl.BlockSpec((1, S, w), lambda b: (b, 0, 0))
    return pl.pallas_call(
        _mlstm_kernel,
        grid=(B,),
        in_specs=[seq(2 * W), seq(2 * W),
                  pl.BlockSpec((None, nc * G, L), lambda b: (b, 0, 0)),
                  _const_spec((nc * G, 1)), _const_spec((1, W))],
        out_specs=seq(W),
        out_shape=jax.ShapeDtypeStruct((B, S, W), BF16),
        scratch_shapes=[pltpu.VMEM((nc * G, L), F32)] * 3 + [pltpu.VMEM((nc, L, G), F32)] * 2,
        compiler_params=_cparams(("parallel",), 40),
        name="mlstm",
    )(mqk, mvo, gates_row, gb, head_g.reshape(1, W))


def _memkv_kernel(mem_ref, g_ref, w_ref, k_ref, v_ref):
    mn = _rms(mem_ref[0], g_ref[...]).astype(BF16)
    k_ref[0] = _dot(mn, w_ref[:, 0:D_MODEL]).astype(BF16)
    v_ref[0] = _dot(mn, w_ref[:, D_MODEL:]).astype(BF16)


def _memkv(mem, g, w):
    B, M, _ = mem.shape
    blk = pl.BlockSpec((1, M, D_MODEL), lambda b: (b, 0, 0))
    return pl.pallas_call(
        _memkv_kernel,
        grid=(B,),
        in_specs=[blk, _const_spec((1, D_MODEL)), _const_spec((D_MODEL, 2 * D_MODEL))],
        out_specs=[blk, blk],
        out_shape=[jax.ShapeDtypeStruct((B, M, D_MODEL), BF16)] * 2,
        compiler_params=_cparams(("parallel",), 32),
        name="memkv",
    )(mem, g, w)


def _mid_kernel(x_ref, attn_ref, hm_ref, wmix_ref, g_ref, wq_ref, k_ref, v_ref, wo_ref, out_ref, o_scr):
    A = ATTN_WIDTH
    h1 = x_ref[0] + _dot(attn_ref[0], wmix_ref[0:A, :]) + _dot(hm_ref[0], wmix_ref[A:, :])
    hn = _rms(h1, g_ref[...]).astype(BF16)
    q = (_dot(hn, wq_ref[...]) * (HEAD_DIM_XATTN ** -0.5)).astype(BF16)
    dh = HEAD_DIM_XATTN
    cols = [slice(hd * dh, (hd + 1) * dh) for hd in range(N_XATTN_HEADS)]
    scores = [_dot_nt(q[:, c], k_ref[0, :, c]) for c in cols]
    soft = []
    for s in scores:
        m = jnp.max(s, axis=-1, keepdims=True)
        pe = jnp.exp(s - m)
        soft.append((pe.astype(BF16), jnp.sum(pe, axis=-1, keepdims=True)))
    pvs = [_dot(pe, v_ref[0, :, c]) for (pe, _), c in zip(soft, cols)]
    for pv, (_, l), c in zip(pvs, soft, cols):
        o_scr[:, c] = (pv / l).astype(BF16)
    out_ref[0] = h1 + _dot(o_scr[...], wo_ref[...])


def _mid(x, attn, hm, wmix, g, wq, k, v, wo):
    B, S, _ = x.shape
    M = k.shape[1]
    tm = ROW_TILE
    rows = lambda w: pl.BlockSpec((1, tm, w), lambda b, i: (b, i, 0))
    kvb = pl.BlockSpec((1, M, D_MODEL), lambda b, i: (b, 0, 0))
    sq = _const_spec((D_MODEL, D_MODEL))
    return pl.pallas_call(
        _mid_kernel,
        grid=(B, S // tm),
        in_specs=[rows(D_MODEL), rows(ATTN_WIDTH), rows(MLSTM_WIDTH), sq,
                  _const_spec((1, D_MODEL)), sq, kvb, kvb, sq],
        out_specs=rows(D_MODEL),
        out_shape=jax.ShapeDtypeStruct((B, S, D_MODEL), F32),
        scratch_shapes=[pltpu.VMEM((tm, D_MODEL), BF16)],
        compiler_params=_cparams(("parallel", "parallel"), 48),
        name="mid",
    )(x, attn, hm, wmix, g, wq, k, v, wo)


def _ffn_kernel(h_ref, g_ref, wup_ref, cw_ref, cb_ref, wd_ref, gf_ref, out_ref, hn_ref, tail_ref,
                *, final_norm):
    tm = h_ref.shape[1]
    FC = FFN_CHUNK
    nchunk = D_FF // FC

    @pl.when(pl.program_id(1) == 0)
    def _():
        tail_ref[...] = jnp.zeros_like(tail_ref)

    hres = h_ref[0]
    hn_ref[...] = _rms(hres, g_ref[...]).astype(BF16)
    out_ref[0] = hres

    def up_dots(c):
        hn = hn_ref[...]
        return _dot(hn, wup_ref[c]), _dot(hn, wup_ref[nchunk + c])

    nxt = up_dots(0)
    for c in range(nchunk):
        gate, up = nxt
        if c + 1 < nchunk:
            nxt = up_dots(c + 1)
        cols = slice(c * FC, (c + 1) * FC)
        cv = cb_ref[:, cols]
        for jt, t in enumerate(_causal_taps(gate, tail_ref[c], FFN_CONV)):
            cv = cv + cw_ref[jt:jt + 1, cols] * t
        tail_ref[c] = gate[tm - SUBLANES:tm, :]
        act = (cv * _sigmoid(cv) * up).astype(BF16)
        out_ref[0] += _dot(act, wd_ref[cols, :])

    if final_norm:
        out_ref[0] = _rms(out_ref[0], gf_ref[...])


def _ffn(h, g, w_up, conv_w, conv_b, w_down, g_final, final_norm):
    B, S, _ = h.shape
    FC = FFN_CHUNK
    nchunk = D_FF // FC
    tm = ROW_TILE
    w_up_c = w_up.reshape(D_MODEL, 2 * nchunk, FC).transpose(1, 0, 2)
    rows = pl.BlockSpec((1, tm, D_MODEL), lambda b, i: (b, i, 0))
    resident = lambda shape: pl.BlockSpec(shape, lambda b, i: (0,) * len(shape),
                                          pipeline_mode=pl.Buffered(1))
    return pl.pallas_call(
        functools.partial(_ffn_kernel, final_norm=final_norm),
        grid=(B, S // tm),
        in_specs=[rows, _const_spec((1, D_MODEL)),
                  resident((2 * nchunk, D_MODEL, FC)),
                  _const_spec((FFN_CONV, D_FF)), _const_spec((1, D_FF)),
                  resident((D_FF, D_MODEL)),
                  _const_spec((1, D_MODEL))],
        out_specs=rows,
        out_shape=jax.ShapeDtypeStruct((B, S, D_MODEL), F32),
        scratch_shapes=[pltpu.VMEM((tm, D_MODEL), BF16),
                        pltpu.VMEM((nchunk, SUBLANES, FC), F32)],
        compiler_params=_cparams(("parallel", "arbitrary"), 48),
        name="ffn",
    )(h, g, w_up_c, conv_w, conv_b, w_down, g_final)


def kernel(x, mem, norm_mix_g, w_in, mlstm_conv_w, mlstm_conv_b, mlstm_gate_b, mlstm_head_g,
           w_mix_out, norm_xattn_g, norm_mem_g, w_xq, w_xkv, w_xo, norm_ffn_g, w_ffn_up,
           ffn_conv_w, ffn_conv_b, w_ffn_down, norm_final_g):
    depth = w_in.shape[0]
    row1 = lambda v: v.reshape(1, -1)
    h = x
    for l in range(depth):
        w_in_p = jnp.pad(w_in[l], ((0, 0), (0, GATE_PAD - 2 * N_MLSTM_HEADS))).astype(BF16)
        aqkv, mqk, mvo, gates = _inproj(h, row1(norm_mix_g[l]), w_in_p,
                                        mlstm_conv_w[l], row1(mlstm_conv_b[l]))
        attn = _attn(aqkv)
        hm = _mlstm(mqk, mvo, gates, mlstm_gate_b[l], mlstm_head_g[l])
        k, v = _memkv(mem, row1(norm_mem_g[l]), w_xkv[l].astype(BF16))
        h = _mid(h, attn, hm, w_mix_out[l].astype(BF16), row1(norm_xattn_g[l]),
                 w_xq[l].astype(BF16), k, v, w_xo[l].astype(BF16))
        h = _ffn(h, row1(norm_ffn_g[l]), w_ffn_up[l].astype(BF16), ffn_conv_w[l],
                 row1(ffn_conv_b[l]), w_ffn_down[l].astype(BF16), row1(norm_final_g),
                 final_norm=l == depth - 1)
    return h
```

```python
import functools
import math

import jax
import jax.numpy as jnp
from jax import lax
from jax.experimental import pallas as pl
from jax.experimental.pallas import tpu as pltpu

F32 = jnp.float32
BF16 = jnp.bfloat16

D_MODEL = 1024
ATTN_WIDTH = 512
HEAD_DIM_ATTN = 64
N_ATTN_HEADS = 8
DILATIONS = (1, 4, 16)
BAND_BLOCK = 128
MLSTM_WIDTH = 512
N_MLSTM_HEADS = 4
HEAD_DIM_MLSTM = 128
MLSTM_CONV = 4
N_XATTN_HEADS = 4
HEAD_DIM_XATTN = 256
D_FF = 2816
FFN_CONV = 3
RMS_EPS = 1e-6
NEG_INF = -1e30
LOG2E = math.log2(math.e)

LANES = 128
SUBLANES = 8
GATE_PAD = LANES
MLSTM_CHUNK = 256
ROW_TILE = 512
FFN_CHUNK = 256
ATTN_GROUP = 4
MIB = 1024 * 1024


def _cparams(semantics, vmem_mib):
    return pltpu.CompilerParams(dimension_semantics=semantics, vmem_limit_bytes=vmem_mib * MIB)


def _rms(x, g):
    return x * lax.rsqrt(jnp.mean(x * x, axis=-1, keepdims=True) + RMS_EPS) * g


def _dot(a, b):
    return jnp.dot(a, b, preferred_element_type=F32)


def _dot_nt(a, b):
    return lax.dot_general(a, b, (((1,), (1,)), ((), ())), preferred_element_type=F32)


def _sigmoid(x):
    return 1.0 / (1.0 + jnp.exp(-x))


def _log_sigmoid(x):
    return jnp.minimum(x, 0.0) - jnp.log(1.0 + jnp.exp(-jnp.abs(x)))


def _const_spec(shape):
    return pl.BlockSpec(shape, lambda *_: (0,) * len(shape))


def _causal_taps(x, tail, n_taps):
    R = x.shape[0]
    row = lax.broadcasted_iota(jnp.int32, x.shape, 0)
    taps = [x]
    for j in range(1, n_taps):
        head = jnp.tile(pltpu.roll(tail, j, 0), (R // SUBLANES, 1))
        taps.append(jnp.where(row < j, head, pltpu.roll(x, j, 0)))
    return taps


def _lane_scan(x, op, fill):
    lane = lax.broadcasted_iota(jnp.int32, x.shape, 1)
    k = 1
    while k < x.shape[1]:
        x = op(x, jnp.where(lane >= k, pltpu.roll(x, k, 1), fill))
        k *= 2
    return x


def _inproj_kernel(x_ref, g_ref, w_ref, cw_ref, cb_ref, aqkv_ref, mqk_ref, mvo_ref, gates_ref, tail_ref):
    tm = x_ref.shape[1]
    A, M = ATTN_WIDTH, MLSTM_WIDTH

    @pl.when(pl.program_id(1) == 0)
    def _():
        tail_ref[...] = jnp.zeros_like(tail_ref)

    xn = _rms(x_ref[0], g_ref[...]).astype(BF16)

    def mm(c0, c1):
        return _dot(xn, w_ref[:, c0:c1])

    o = 3 * A
    raws = [mm(o, o + M), mm(o + M, o + 2 * M)]
    tails = [tail_ref[:, 0:M], tail_ref[:, M:2 * M]]
    mo = mm(o + 3 * M, o + 4 * M)
    mv = mm(o + 2 * M, o + 3 * M)
    aq, ak, av = mm(0, A), mm(A, 2 * A), mm(2 * A, 3 * A)
    gates = mm(o + 4 * M, o + 4 * M + GATE_PAD)
    for half, raw in enumerate(raws):
        cols = slice(half * M, (half + 1) * M)
        acc = cb_ref[:, cols]
        for j, t in enumerate(_causal_taps(raw, tails[half], MLSTM_CONV)):
            acc = acc + cw_ref[j:j + 1, cols] * t
        y = acc * _sigmoid(acc)
        if half == 1:
            y = y * (HEAD_DIM_MLSTM ** -0.5)
        mqk_ref[0, :, cols] = y.astype(BF16)
        tail_ref[:, cols] = raw[tm - SUBLANES:tm, :]
    mvo_ref[0, :, M:2 * M] = _sigmoid(mo).astype(BF16)
    mvo_ref[0, :, 0:M] = mv.astype(BF16)
    aqkv_ref[0, :, 0:A] = aq * (HEAD_DIM_ATTN ** -0.5 * LOG2E)
    aqkv_ref[0, :, A:2 * A] = ak
    aqkv_ref[0, :, 2 * A:3 * A] = av
    gates_ref[0] = gates


def _inproj(x, g, w, conv_w, conv_b):
    B, S, _ = x.shape
    tm = ROW_TILE
    ncol = w.shape[1]
    M2 = 2 * MLSTM_WIDTH
    rows = lambda wd: pl.BlockSpec((1, tm, wd), lambda b, i: (b, i, 0))
    return pl.pallas_call(
        _inproj_kernel,
        grid=(B, S // tm),
        in_specs=[rows(D_MODEL), _const_spec((1, D_MODEL)), _const_spec((D_MODEL, ncol)),
                  _const_spec((MLSTM_CONV, M2)), _const_spec((1, M2))],
        out_specs=[rows(3 * ATTN_WIDTH), rows(M2), rows(M2), rows(GATE_PAD)],
        out_shape=[jax.ShapeDtypeStruct((B, S, 3 * ATTN_WIDTH), F32),
                   jax.ShapeDtypeStruct((B, S, M2), BF16),
                   jax.ShapeDtypeStruct((B, S, M2), BF16),
                   jax.ShapeDtypeStruct((B, S, GATE_PAD), F32)],
        scratch_shapes=[pltpu.VMEM((SUBLANES, M2), F32)],
        compiler_params=_cparams(("parallel", "arbitrary"), 48),
        name="inproj",
    )(x, g, w, conv_w, conv_b)


def _alibi_tables():
    qi = jnp.arange(BAND_BLOCK)[:, None]
    kj = jnp.arange(2 * BAND_BLOCK)[None, :]
    rel = BAND_BLOCK + qi - kj
    valid = (rel >= 0) & (rel <= BAND_BLOCK)
    h = jnp.arange(1, N_ATTN_HEADS + 1, dtype=F32)
    slopes = jnp.exp2(-8.0 * h / N_ATTN_HEADS)
    tabs = []
    for d in DILATIONS:
        dist = (rel * d).astype(F32)
        tabs.append(jnp.where(valid[None], -slopes[:, None, None] * dist[None] * LOG2E, NEG_INF))
    return jnp.stack(tabs)


def _attn_kernel(q_ref, k_ref, v_ref, tab_ref, out_ref, q_s, k_s, v_s, acc_s, m_s, l_s):
    S = q_ref.shape[1]
    B = BAND_BLOCK
    lo = lax.broadcasted_iota(jnp.int32, (B, LANES), 1) < HEAD_DIM_ATTN

    for bi, d in enumerate(DILATIONS):
        U = S // d
        for r in range(d):
            src = pl.ds(r, U, stride=d) if d > 1 else pl.ds(0, S)
            dst = pl.ds(r * U, U)
            q_s[bi, dst, :] = q_ref[0, src, :].astype(BF16)
            k_s[bi, dst, :] = k_ref[0, src, :].astype(BF16)
            v_s[bi, dst, :] = v_ref[0, src, :].astype(BF16)

    def group(bi, blks):
        d = DILATIONS[bi]
        scores = []
        for row0, _, has_prev in blks:
            qrow = pl.ds(row0, B)
            krow = pl.ds(row0 - B, 2 * B) if has_prev else qrow
            q2 = q_s[bi, qrow, :]
            k2 = k_s[bi, krow, :]
            v2 = v_s[bi, krow, :]
            zero = jnp.zeros_like(q2)
            scores.append((_dot_nt(jnp.where(lo, q2, zero), k2), v2, 0, has_prev))
            scores.append((_dot_nt(jnp.where(lo, zero, q2), k2), v2, 1, has_prev))
        soft = []
        for s, v2, e, has_prev in scores:
            s = s + (tab_ref[bi, e] if has_prev else tab_ref[bi, e, :, B:2 * B])
            m = jnp.max(s, axis=-1, keepdims=True)
            pe = jnp.exp2(s - m)
            soft.append((m, jnp.sum(pe, axis=-1, keepdims=True), pe.astype(BF16), v2))
        pvs = [_dot(pe, v2) for _, _, pe, v2 in soft]
        res = []
        for i, (_, start, _) in enumerate(blks):
            m = jnp.where(lo, soft[2 * i][0], soft[2 * i + 1][0])
            l = jnp.where(lo, soft[2 * i][1], soft[2 * i + 1][1])
            pv = jnp.where(lo, pvs[2 * i], pvs[2 * i + 1])
            pos = pl.ds(start, B, stride=d) if d > 1 else pl.ds(start, B)
            if bi > 0:
                m_old = m_s[pos, :]
                m_new = jnp.maximum(m_old, m)
                a_old = jnp.exp2(m_old - m_new)
                a_blk = jnp.exp2(m - m_new)
                l = a_old * l_s[pos, :] + a_blk * l
                pv = a_old * acc_s[pos, :] + a_blk * pv
                m = m_new
            res.append((pos, m, l, pv))
        return res

    for bi, d in enumerate(DILATIONS):
        U = S // d
        blks = [(r * U + n * B, r + n * B * d, n > 0) for r in range(d) for n in range(U // B)]
        res = []
        for g in range(0, len(blks), ATTN_GROUP):
            res += group(bi, blks[g:g + ATTN_GROUP])
        for pos, m, l, pv in res:
            m_s[pos, :] = m
            l_s[pos, :] = l
            acc_s[pos, :] = pv

    out_ref[0] = (acc_s[...] / l_s[...]).astype(BF16)


def _attn(aqkv):
    B, S, _ = aqkv.shape
    npair = N_ATTN_HEADS // 2
    assert S == 16 * BAND_BLOCK, "block schedule below is written for 16 blocks of 128"
    tabs = _alibi_tables()
    seq = lambda off: pl.BlockSpec((1, S, LANES), lambda b, p: (b, 0, p + off))
    return pl.pallas_call(
        _attn_kernel,
        grid=(B, npair),
        in_specs=[seq(0), seq(npair), seq(2 * npair),
                  pl.BlockSpec((len(DILATIONS), 2, BAND_BLOCK, 2 * BAND_BLOCK), lambda b, p: (0, p, 0, 0))],
        out_specs=seq(0),
        out_shape=jax.ShapeDtypeStruct((B, S, ATTN_WIDTH), BF16),
        scratch_shapes=[pltpu.VMEM((len(DILATIONS), S, LANES), BF16)] * 3
                       + [pltpu.VMEM((S, LANES), F32)] * 3,
        compiler_params=_cparams(("parallel", "parallel"), 32),
        name="attn",
    )(aqkv, aqkv, aqkv, tabs)


def _mlstm_kernel(qk_ref, vo_ref, gr_ref, gb_ref, hg_ref, out_ref, rv_s, cm_s, b_s, cols1_s, cols2_s):
    S = qk_ref.shape[1]
    L = MLSTM_CHUNK
    nc = S // L
    H, dh, W = N_MLSTM_HEADS, HEAD_DIM_MLSTM, MLSTM_WIDTH
    G = 2 * H
    ti = lax.broadcasted_iota(jnp.int32, (L, L), 0)
    si = lax.broadcasted_iota(jnp.int32, (L, L), 1)
    causal = si <= ti

    X = gr_ref[...] + gb_ref[...]
    is_li = (lax.broadcasted_iota(jnp.int32, X.shape, 0) & (G - 1)) < H
    Bc = _lane_scan(_log_sigmoid(X), jnp.add, 0.0)
    Bup = pltpu.roll(Bc, nc * G - H, 0)
    RV = X - Bup
    CM = _lane_scan(RV, jnp.maximum, NEG_INF)
    A1 = jnp.where(is_li, RV, pltpu.roll(CM, H, 0))
    rv_s[...] = RV
    cm_s[...] = CM
    b_s[...] = Bup
    for c in range(nc):
        cols1_s[c] = A1[c * G:(c + 1) * G, :].T
        cols2_s[c] = Bup[c * G:(c + 1) * G, :].T

    def chunk(c, carry):
        rows = pl.ds(pl.multiple_of(c * L, L), L)
        grow = pl.ds(pl.multiple_of(c * G, G), G)
        RV = rv_s[grow, :]
        CM = cm_s[grow, :]
        Bup = b_s[grow, :]
        cols1 = cols1_s[c]
        cols2 = cols2_s[c]
        prep, new_carry = [], []
        ones = jnp.ones((L, dh), BF16)
        for h in range(H):
            Cn, m = carry[h]
            col = slice(h * dh, (h + 1) * dh)
            qc = qk_ref[0, rows, col]
            kc = qk_ref[0, rows, W + h * dh:W + (h + 1) * dh]
            v1 = jnp.concatenate([vo_ref[0, rows, col], ones], axis=1)
            qk = _dot_nt(qc, kc)
            qCn = _dot(qc, Cn.astype(BF16))
            g = Bup[h:h + 1, L - 1:L]
            cm_last = CM[h:h + 1, L - 1:L]
            kwT = kc.astype(F32).T * jnp.exp(RV[h:h + 1, :] - cm_last)
            kvn = _dot(kwT.astype(BF16), v1)
            prep.append((v1, qk, qCn, m))
            m_loc = g + cm_last
            m_new = jnp.maximum(g + m, m_loc)
            new_carry.append((jnp.exp(g + m - m_new) * Cn + jnp.exp(m_loc - m_new) * kvn, m_new))
        mid = []
        for h, (v1, qk, qCn, m) in enumerate(prep):
            mm = jnp.maximum(jnp.broadcast_to(cols1[:, H + h:H + h + 1], (L, dh)), m)
            b_t = jnp.broadcast_to(cols2[:, h:h + 1], (L, dh))
            mm2 = jnp.concatenate([mm] * (L // dh), axis=1)
            Sm = jnp.exp(jnp.where(causal, RV[h:h + 1, :] - mm2, NEG_INF)) * qk
            intra = _dot(Sm.astype(BF16), v1)
            inter_w = jnp.exp(m - mm)
            den = inter_w * qCn[:, dh:] + intra[:, dh:]
            mid.append((inter_w * qCn[:, :dh] + intra[:, :dh],
                        jnp.maximum(jnp.abs(den), jnp.exp(-(b_t + mm)))))
        for h, (num, dn) in enumerate(mid):
            col = slice(h * dh, (h + 1) * dh)
            hh = num / dn
            hn = hh * lax.rsqrt(jnp.mean(hh * hh, axis=-1, keepdims=True) + RMS_EPS) * hg_ref[:, col]
            og = vo_ref[0, rows, W + h * dh:W + (h + 1) * dh].astype(F32)
            out_ref[0, rows, col] = (hn * og).astype(BF16)
        return tuple(new_carry)

    init = tuple((jnp.zeros((dh, 2 * dh), F32), jnp.zeros((1, 1), F32)) for _ in range(H))
    lax.fori_loop(0, nc, chunk, init, unroll=2)


def _mlstm(mqk, mvo, gates, gate_b, head_g):
    B, S, _ = mqk.shape
    H, L, W = N_MLSTM_HEADS, MLSTM_CHUNK, MLSTM_WIDTH
    nc = S // L
    G = 2 * H
    gates_row = gates[:, :, :G].reshape(B, nc, L, G).transpose(0, 1, 3, 2).reshape(B, nc * G, L)
    gb = jnp.tile(gate_b.reshape(G, 1), (nc, 1))
    seq = lambda w: pl.BlockSpec((1, S, w), lambda b: (b, 0, 0))
    return pl.pallas_call(
        _mlstm_kernel,
        grid=(B,),
        in_specs=[seq(2 * W), seq(2 * W),
                  pl.BlockSpec((None, nc * G, L), lambda b: (b, 0, 0)),
                  _const_spec((nc * G, 1)), _const_spec((1, W))],
        out_specs=seq(W),
        out_shape=jax.ShapeDtypeStruct((B, S, W), BF16),
        scratch_shapes=[pltpu.VMEM((nc * G, L), F32)] * 3 + [pltpu.VMEM((nc, L, G), F32)] * 2,
        compiler_params=_cparams(("parallel",), 40),
        name="mlstm",
    )(mqk, mvo, gates_row, gb, head_g.reshape(1, W))


def _memkv_kernel(mem_ref, g_ref, w_ref, k_ref, v_ref):
    mn = _rms(mem_ref[0], g_ref[...]).astype(BF16)
    k_ref[0] = _dot(mn, w_ref[:, 0:D_MODEL]).astype(BF16)
    v_ref[0] = _dot(mn, w_ref[:, D_MODEL:]).astype(BF16)


def _memkv(mem, g, w):
    B, M, _ = mem.shape
    blk = pl.BlockSpec((1, M, D_MODEL), lambda b: (b, 0, 0))
    return pl.pallas_call(
        _memkv_kernel,
        grid=(B,),
        in_specs=[blk, _const_spec((1, D_MODEL)), _const_spec((D_MODEL, 2 * D_MODEL))],
        out_specs=[blk, blk],
        out_shape=[jax.ShapeDtypeStruct((B, M, D_MODEL), BF16)] * 2,
        compiler_params=_cparams(("parallel",), 32),
        name="memkv",
    )(mem, g, w)


def _mid_kernel(x_ref, attn_ref, hm_ref, wmix_ref, g_ref, wq_ref, k_ref, v_ref, wo_ref, out_ref, o_scr):
    A = ATTN_WIDTH
    h1 = x_ref[0] + _dot(attn_ref[0], wmix_ref[0:A, :]) + _dot(hm_ref[0], wmix_ref[A:, :])
    hn = _rms(h1, g_ref[...]).astype(BF16)
    q = (_dot(hn, wq_ref[...]) * (HEAD_DIM_XATTN ** -0.5)).astype(BF16)
    dh = HEAD_DIM_XATTN
    cols = [slice(hd * dh, (hd + 1) * dh) for hd in range(N_XATTN_HEADS)]
    scores = [_dot_nt(q[:, c], k_ref[0, :, c]) for c in cols]
    soft = []
    for s in scores:
        m = jnp.max(s, axis=-1, keepdims=True)
        pe = jnp.exp(s - m)
        soft.append((pe.astype(BF16), jnp.sum(pe, axis=-1, keepdims=True)))
    pvs = [_dot(pe, v_ref[0, :, c]) for (pe, _), c in zip(soft, cols)]
    for pv, (_, l), c in zip(pvs, soft, cols):
        o_scr[:, c] = (pv / l).astype(BF16)
    out_ref[0] = h1 + _dot(o_scr[...], wo_ref[...])


def _mid(x, attn, hm, wmix, g, wq, k, v, wo):
    B, S, _ = x.shape
    M = k.shape[1]
    tm = ROW_TILE
    rows = lambda w: pl.BlockSpec((1, tm, w), lambda b, i: (b, i, 0))
    kvb = pl.BlockSpec((1, M, D_MODEL), lambda b, i: (b, 0, 0))
    sq = _const_spec((D_MODEL, D_MODEL))
    return pl.pallas_call(
        _mid_kernel,
        grid=(B, S // tm),
        in_specs=[rows(D_MODEL), rows(ATTN_WIDTH), rows(MLSTM_WIDTH), sq,
                  _const_spec((1, D_MODEL)), sq, kvb, kvb, sq],
        out_specs=rows(D_MODEL),
        out_shape=jax.ShapeDtypeStruct((B, S, D_MODEL), F32),
        scratch_shapes=[pltpu.VMEM((tm, D_MODEL), BF16)],
        compiler_params=_cparams(("parallel", "parallel"), 48),
        name="mid",
    )(x, attn, hm, wmix, g, wq, k, v, wo)


def _ffn_kernel(h_ref, g_ref, wup_ref, cw_ref, cb_ref, wd_ref, gf_ref, out_ref, tail_ref,
                *, final_norm):
    tm = h_ref.shape[1]
    FC = FFN_CHUNK
    nchunk = D_FF // FC

    @pl.when(pl.program_id(1) == 0)
    def _():
        tail_ref[...] = jnp.zeros_like(tail_ref)

    hres = h_ref[0]
    hn = _rms(hres, g_ref[...]).astype(BF16)
    tails = [tail_ref[c] for c in range(nchunk)]

    def up_dots(c):
        return _dot(hn, wup_ref[c]), _dot(hn, wup_ref[nchunk + c])

    nxt = up_dots(0)
    acc = hres
    for c in range(nchunk):
        gate, up = nxt
        if c + 1 < nchunk:
            nxt = up_dots(c + 1)
        cols = slice(c * FC, (c + 1) * FC)
        cv = cb_ref[:, cols]
        for jt, t in enumerate(_causal_taps(gate, tails[c], FFN_CONV)):
            cv = cv + cw_ref[jt:jt + 1, cols] * t
        tails[c] = gate[tm - SUBLANES:tm, :]
        act = (cv * _sigmoid(cv) * up).astype(BF16)
        acc = acc + _dot(act, wd_ref[cols, :])

    for c in range(nchunk):
        tail_ref[c] = tails[c]
    out_ref[0] = _rms(acc, gf_ref[...]) if final_norm else acc


def _ffn(h, g, w_up, conv_w, conv_b, w_down, g_final, final_norm):
    B, S, _ = h.shape
    FC = FFN_CHUNK
    nchunk = D_FF // FC
    tm = ROW_TILE
    w_up_c = w_up.reshape(D_MODEL, 2 * nchunk, FC).transpose(1, 0, 2)
    rows = pl.BlockSpec((1, tm, D_MODEL), lambda b, i: (b, i, 0))
    resident = lambda shape: pl.BlockSpec(shape, lambda b, i: (0,) * len(shape),
                                          pipeline_mode=pl.Buffered(1))
    return pl.pallas_call(
        functools.partial(_ffn_kernel, final_norm=final_norm),
        grid=(B, S // tm),
        in_specs=[rows, _const_spec((1, D_MODEL)),
                  resident((2 * nchunk, D_MODEL, FC)),
                  _const_spec((FFN_CONV, D_FF)), _const_spec((1, D_FF)),
                  resident((D_FF, D_MODEL)),
                  _const_spec((1, D_MODEL))],
        out_specs=rows,
        out_shape=jax.ShapeDtypeStruct((B, S, D_MODEL), F32),
        scratch_shapes=[pltpu.VMEM((nchunk, SUBLANES, FC), F32)],
        compiler_params=_cparams(("parallel", "arbitrary"), 48),
        name="ffn",
    )(h, g, w_up_c, conv_w, conv_b, w_down, g_final)


def kernel(x, mem, norm_mix_g, w_in, mlstm_conv_w, mlstm_conv_b, mlstm_gate_b, mlstm_head_g,
           w_mix_out, norm_xattn_g, norm_mem_g, w_xq, w_xkv, w_xo, norm_ffn_g, w_ffn_up,
           ffn_conv_w, ffn_conv_b, w_ffn_down, norm_final_g):
    depth = w_in.shape[0]
    row1 = lambda v: v.reshape(1, -1)
    h = x
    for l in range(depth):
        w_in_p = jnp.pad(w_in[l], ((0, 0), (0, GATE_PAD - 2 * N_MLSTM_HEADS))).astype(BF16)
        aqkv, mqk, mvo, gates = _inproj(h, row1(norm_mix_g[l]), w_in_p,
                                        mlstm_conv_w[l], row1(mlstm_conv_b[l]))
        attn = _attn(aqkv)
        hm = _mlstm(mqk, mvo, gates, mlstm_gate_b[l], mlstm_head_g[l])
        k, v = _memkv(mem, row1(norm_mem_g[l]), w_xkv[l].astype(BF16))
        h = _mid(h, attn, hm, w_mix_out[l].astype(BF16), row1(norm_xattn_g[l]),
                 w_xq[l].astype(BF16), k, v, w_xo[l].astype(BF16))
        h = _ffn(h, row1(norm_ffn_g[l]), w_ffn_up[l].astype(BF16), ffn_conv_w[l],
                 row1(ffn_conv_b[l]), w_ffn_down[l].astype(BF16), row1(norm_final_g),
                 final_norm=l == depth - 1)
    return h
```
